```python
import math
import jax, jax.numpy as jnp
from jax import lax
import numpy as np

D_MODEL = 1024
BATCH = 16
SEQ = 4096
DEPTH = 4

N_MIXERS = 2
N_LRU = (DEPTH + 1) // 2
N_RET = DEPTH // 2

D_RNN = 3 * D_MODEL // 2
N_GATE_BLOCKS = 16
BLOCK_W = D_RNN // N_GATE_BLOCKS
CONV_W = 4
C_RG = 8.0

RET_HEADS = D_MODEL // 256
QK_HEAD = 256
V_HEAD = 2 * QK_HEAD
QK_DIM = RET_HEADS * QK_HEAD
V_DIM = RET_HEADS * V_HEAD
CHUNK = 128
ROPE_BASE = 10000.0

RMS_EPS = 1e-6
GN_EPS = 1e-5

kernel_name = "hybrid_rglru_retention_sandwich"


def rms_norm(x, g):
    xf = x.astype(jnp.float32)
    y = xf * lax.rsqrt(jnp.mean(xf * xf, axis=-1, keepdims=True) + RMS_EPS)
    return (y * g.astype(jnp.float32)).astype(x.dtype)


def causal_depthwise_conv(x, w, b):
    s = x.shape[1]
    xp = jnp.pad(x, ((0, 0), (CONV_W - 1, 0), (0, 0)))
    y = b
    for k in range(CONV_W):
        y = y + xp[:, k:k + s] * w[k]
    return y


def _lin_rec_combine(e1, e2):
    a1, b1 = e1
    a2, b2 = e2
    return a1 * a2, a2 * b1 + b2


def rglru_mixer(h, w_in, conv_w, conv_b, w_a, b_a, w_x, b_x, lam, w_out):
    bsz, s, _ = h.shape
    u = h @ w_in
    xb, gate = u[..., :D_RNN], u[..., D_RNN:]
    xb = causal_depthwise_conv(xb, conv_w, conv_b)
    xblk = xb.reshape(bsz, s, N_GATE_BLOCKS, BLOCK_W)
    r = jax.nn.sigmoid(jnp.einsum('bsnc,ncd->bsnd', xblk, w_a).reshape(bsz, s, D_RNN) + b_a)
    i = jax.nn.sigmoid(jnp.einsum('bsnc,ncd->bsnd', xblk, w_x).reshape(bsz, s, D_RNN) + b_x)
    log_a = -C_RG * r.astype(jnp.float32) * jax.nn.softplus(-lam.astype(jnp.float32))
    a = jnp.exp(log_a)
    bt = jnp.sqrt(-jnp.expm1(2.0 * log_a)) * (i * xb).astype(jnp.float32)
    _, hseq = lax.associative_scan(_lin_rec_combine, (a, bt), axis=1)
    y = hseq.astype(h.dtype) * jax.nn.silu(gate)
    return y @ w_out


def apply_rotary(x, cos, sin):
    half = x.shape[-1] // 2
    x1, x2 = x[..., :half], x[..., half:]
    c = cos[None, :, None, :]
    s_ = sin[None, :, None, :]
    return jnp.concatenate([x1 * c - x2 * s_, x2 * c + x1 * s_], axis=-1)


def retention_mixer(h, w_in, w_out, cos, sin):
    bsz, s, _ = h.shape
    n_chunks = s // CHUNK
    u = h @ w_in
    q = u[..., :QK_DIM].reshape(bsz, s, RET_HEADS, QK_HEAD).astype(jnp.float32)
    k = u[..., QK_DIM:2 * QK_DIM].reshape(bsz, s, RET_HEADS, QK_HEAD).astype(jnp.float32)
    v = u[..., 2 * QK_DIM:2 * QK_DIM + V_DIM].reshape(bsz, s, RET_HEADS, V_HEAD).astype(jnp.float32)
    gate = u[..., 2 * QK_DIM + V_DIM:]
    q = apply_rotary(q, cos, sin)
    k = apply_rotary(k, cos, sin) * (QK_HEAD ** -0.5)

    def to_chunks(t):
        return t.reshape(bsz, n_chunks, CHUNK, RET_HEADS, t.shape[-1]).transpose(1, 0, 3, 2, 4)

    qc, kc, vc = to_chunks(q), to_chunks(k), to_chunks(v)

    log_g = jnp.log(1.0 - jnp.exp2(-5.0 - jnp.arange(RET_HEADS, dtype=jnp.float32)))
    idx = jnp.arange(CHUNK, dtype=jnp.float32)
    diff = idx[:, None] - idx[None, :]
    dmask = jnp.where(diff[None] >= 0, jnp.exp(jnp.maximum(diff, 0.0)[None] * log_g[:, None, None]), 0.0)
    q_decay = jnp.exp((idx[None, :] + 1.0) * log_g[:, None])[..., None]
    k_decay = jnp.exp((CHUNK - 1.0 - idx[None, :]) * log_g[:, None])[..., None]
    chunk_decay = jnp.exp(CHUNK * log_g)[:, None, None]

    def step(state, inp):
        qi, ki, vi = inp
        scores = jnp.einsum('bhqd,bhkd->bhqk', qi, ki) * dmask
        inner = jnp.einsum('bhqk,bhke->bhqe', scores, vi)
        cross = jnp.einsum('bhqd,bhde->bhqe', qi * q_decay, state)
        new_state = state * chunk_decay + jnp.einsum('bhkd,bhke->bhde', ki * k_decay, vi)
        return new_state, inner + cross

    state0 = jnp.zeros((bsz, RET_HEADS, QK_HEAD, V_HEAD), jnp.float32)
    _, out = lax.scan(step, state0, (qc, kc, vc))
    out = out.transpose(1, 0, 3, 2, 4).reshape(bsz, s, RET_HEADS, V_HEAD)
    mu = jnp.mean(out, axis=-1, keepdims=True)
    var = jnp.mean(jnp.square(out - mu), axis=-1, keepdims=True)
    out = ((out - mu) * lax.rsqrt(var + GN_EPS)).reshape(bsz, s, V_DIM).astype(h.dtype)
    y = out * jax.nn.silu(gate)
    return y @ w_out


def setup_inputs(seed: int = 0) -> dict:
    key = jax.random.key(seed)
    ks = jax.random.split(key, 16)
    f32 = jnp.float32

    def nrm(k, shape, fan_in):
        return jax.random.normal(k, shape, f32) * (fan_in ** -0.5)

    x = jax.random.normal(ks[0], (BATCH, SEQ, D_MODEL), f32)
    norm_pre = 1.0 + 0.05 * jax.random.normal(ks[1], (DEPTH, D_MODEL), f32)
    norm_post = 1.0 + 0.05 * jax.random.normal(ks[2], (DEPTH, D_MODEL), f32)
    lru_w_in = nrm(ks[3], (N_LRU, D_MODEL, 2 * D_RNN), D_MODEL)
    lru_conv_w = nrm(ks[4], (N_LRU, CONV_W, D_RNN), CONV_W)
    lru_conv_b = 0.02 * jax.random.normal(ks[5], (N_LRU, D_RNN), f32)
    lru_w_a = nrm(ks[6], (N_LRU, N_GATE_BLOCKS, BLOCK_W, BLOCK_W), BLOCK_W)
    lru_b_a = 0.02 * jax.random.normal(ks[7], (N_LRU, D_RNN), f32)
    lru_w_x = nrm(ks[8], (N_LRU, N_GATE_BLOCKS, BLOCK_W, BLOCK_W), BLOCK_W)
    lru_b_x = 0.02 * jax.random.normal(ks[9], (N_LRU, D_RNN), f32)
    a_c = jax.random.uniform(ks[10], (N_LRU, D_RNN), f32, minval=0.9, maxval=0.999)
    sig = a_c ** (1.0 / C_RG)
    lru_lambda = jnp.log(sig) - jnp.log1p(-sig)
    lru_w_out = nrm(ks[11], (N_LRU, D_RNN, D_MODEL), D_RNN)
    ret_w_in = nrm(ks[12], (N_RET, D_MODEL, 2 * QK_DIM + 2 * V_DIM), D_MODEL)
    ret_w_out = nrm(ks[13], (N_RET, V_DIM, D_MODEL), V_DIM)
    return {"x": x, "norm_pre": norm_pre, "norm_post": norm_post,
            "lru_w_in": lru_w_in, "lru_conv_w": lru_conv_w, "lru_conv_b": lru_conv_b,
            "lru_w_a": lru_w_a, "lru_b_a": lru_b_a, "lru_w_x": lru_w_x, "lru_b_x": lru_b_x,
            "lru_lambda": lru_lambda, "lru_w_out": lru_w_out,
            "ret_w_in": ret_w_in, "ret_w_out": ret_w_out}


def reference(x, norm_pre, norm_post, lru_w_in, lru_conv_w, lru_conv_b, lru_w_a, lru_b_a,
              lru_w_x, lru_b_x, lru_lambda, lru_w_out, ret_w_in, ret_w_out):
    s = x.shape[1]
    pos = jnp.arange(s, dtype=jnp.float32)
    inv_freq = ROPE_BASE ** (-jnp.arange(0, QK_HEAD, 2, dtype=jnp.float32) / QK_HEAD)
    ang = pos[:, None] * inv_freq[None, :]
    cos, sin = jnp.cos(ang), jnp.sin(ang)
    for layer in range(DEPTH):
        h = rms_norm(x, norm_pre[layer])
        j = layer // N_MIXERS
        if layer % N_MIXERS == 0:
            o = rglru_mixer(h, lru_w_in[j], lru_conv_w[j], lru_conv_b[j], lru_w_a[j], lru_b_a[j],
                            lru_w_x[j], lru_b_x[j], lru_lambda[j], lru_w_out[j])
        else:
            o = retention_mixer(h, ret_w_in[j], ret_w_out[j], cos, sin)
        x = x + rms_norm(o, norm_post[layer])
    return x
```

```python
import functools

import jax
import jax.numpy as jnp
from jax import lax
from jax.experimental import pallas as pl
from jax.experimental.pallas import tpu as pltpu

D_MODEL = 1024
DEPTH = 4
D_RNN = 1536
N_GATE_BLOCKS = 16
BLOCK_W = 96
CONV_W = 4
C_RG = 8.0
RET_HEADS = 4
QK_HEAD = 256
V_HEAD = 512
QK_DIM = RET_HEADS * QK_HEAD
V_DIM = RET_HEADS * V_HEAD
CHUNK = 128
ROPE_BASE = 10000.0
RMS_EPS = 1e-6
GN_EPS = 1e-5

GATE_GROUP_BLOCKS = 4
GATE_GROUP_W = GATE_GROUP_BLOCKS * BLOCK_W
N_GATE_GROUPS = N_GATE_BLOCKS // GATE_GROUP_BLOCKS
HEAD_COLS = 2 * QK_HEAD + 2 * V_HEAD
SUBLANES = 8
ROW_STEP = 16

LRU_TS = 512
RET_TS = 512
VMEM_LIMIT = 56 * 1024 * 1024

F32 = jnp.float32
BF16 = jnp.bfloat16


def _sigmoid(v):
    return 0.5 * jnp.tanh(0.5 * v) + 0.5


def _rms_scale(v):
    return lax.rsqrt(jnp.mean(v * v, axis=-1, keepdims=True) + RMS_EPS)


def _shift_rows(v, d, fill, rowid):
    return jnp.where(rowid >= d, pltpu.roll(v, d, axis=0), fill)


def _scan8(a, b, rowid):
    for d in (1, 2, 4):
        b = a * _shift_rows(b, d, 0.0, rowid) + b
        a = a * _shift_rows(a, d, 1.0, rowid)
    return a, b


def _lru_kernel(x_ref, gpre_ref, gpost_ref, win_ref, convw_ref, convb_ref, wg_ref, ba_ref, bx_ref,
                lam_ref, wout_ref, o_ref, xp_ref, gate_ref, xb_ref, xb16_ref, g_ref, y_ref, hc_ref):
    ts = x_ref.shape[1]
    t = pl.program_id(1)

    @pl.when(t == 0)
    def _():
        xp_ref[0:SUBLANES, :] = jnp.zeros((SUBLANES, D_RNN), F32)
        hc_ref[...] = jnp.zeros_like(hc_ref)

    x = x_ref[0]
    h = (x * _rms_scale(x) * gpre_ref[...]).astype(BF16)
    xp_ref[SUBLANES:, :] = jnp.dot(h, win_ref[:, :D_RNN], preferred_element_type=F32)
    gate_ref[...] = jnp.dot(h, win_ref[:, D_RNN:], preferred_element_type=F32)

    rc = 32

    def conv_body(j, carry):
        r0 = pl.multiple_of(j * rc, rc)
        win = xp_ref[pl.ds(r0, rc + SUBLANES), :]
        acc = convb_ref[...] + convw_ref[0:1, :] * win[SUBLANES - 3:SUBLANES - 3 + rc]
        for k in range(1, CONV_W):
            off = SUBLANES - (CONV_W - 1) + k
            acc = acc + convw_ref[k:k + 1, :] * win[off:off + rc]
        xb_ref[pl.ds(r0, rc), :] = acc
        xb16_ref[pl.ds(r0, rc), :] = acc.astype(BF16)
        return carry

    lax.fori_loop(0, ts // rc, conv_body, 0)
    xp_ref[0:SUBLANES, :] = xp_ref[ts:ts + SUBLANES, :]

    for g in range(N_GATE_GROUPS):
        g_ref[:, 2 * GATE_GROUP_W * g:2 * GATE_GROUP_W * (g + 1)] = jnp.dot(
            xb16_ref[:, GATE_GROUP_W * g:GATE_GROUP_W * (g + 1)], wg_ref[g],
            preferred_element_type=F32)

    rowid = lax.broadcasted_iota(jnp.int32, (SUBLANES, GATE_GROUP_W), 0)
    lam = lam_ref[...]
    neg_c_sp = -C_RG * (jnp.maximum(-lam, 0.0) + jnp.log1p(jnp.exp(-jnp.abs(lam))))

    def scan_body(j, carries):
        r0 = pl.multiple_of(j * ROW_STEP, ROW_STEP)
        rows = pl.ds(r0, ROW_STEP)
        new = []
        for g in range(N_GATE_GROUPS):
            cs = slice(GATE_GROUP_W * g, GATE_GROUP_W * (g + 1))
            ga = g_ref[rows, 2 * GATE_GROUP_W * g:2 * GATE_GROUP_W * g + GATE_GROUP_W]
            gx = g_ref[rows, 2 * GATE_GROUP_W * g + GATE_GROUP_W:2 * GATE_GROUP_W * (g + 1)]
            xb = xb_ref[rows, cs]
            r = _sigmoid(ga + ba_ref[:, cs])
            i = _sigmoid(gx + bx_ref[:, cs])
            log_a = r * neg_c_sp[:, cs]
            a = jnp.exp(log_a)
            bt = jnp.sqrt(1.0 - a * a) * (i * xb)
            hc = carries[g]
            hs = []
            for s in range(ROW_STEP // SUBLANES):
                sl = slice(SUBLANES * s, SUBLANES * (s + 1))
                aa, bb = _scan8(a[sl], bt[sl], rowid)
                hv = aa * hc + bb
                hc = jnp.broadcast_to(hv[SUBLANES - 1:SUBLANES, :], hv.shape)
                hs.append(hv)
            hseq = jnp.concatenate(hs, axis=0)
            gate = gate_ref[rows, cs]
            y_ref[rows, cs] = (hseq * (gate * _sigmoid(gate))).astype(BF16)
            new.append(hc)
        return tuple(new)

    init = tuple(hc_ref[:, GATE_GROUP_W * g:GATE_GROUP_W * (g + 1)] for g in range(N_GATE_GROUPS))
    fin = lax.fori_loop(0, ts // ROW_STEP, scan_body, init)
    for g in range(N_GATE_GROUPS):
        hc_ref[:, GATE_GROUP_W * g:GATE_GROUP_W * (g + 1)] = fin[g]

    o = jnp.dot(y_ref[...], wout_ref[...], preferred_element_type=F32)
    o_ref[0] = x_ref[0] + o * _rms_scale(o) * gpost_ref[...]


def _ret_kernel(x_ref, gpre_ref, gpost_ref, win_ref, wout_ref, cos_ref, sin_ref, dmask_ref, qdec_ref,
                kdec_ref, cdec_ref, o_ref, h_ref, u_ref, y_ref, state_ref):
    ts = x_ref.shape[1]
    t = pl.program_id(1)

    @pl.when(t == 0)
    def _():
        state_ref[...] = jnp.zeros_like(state_ref)

    x = x_ref[0]
    h_ref[...] = (x * _rms_scale(x) * gpre_ref[...]).astype(BF16)
    half = QK_HEAD // 2
    for hd in range(RET_HEADS):
        u_ref[...] = jnp.dot(h_ref[...], win_ref[hd], preferred_element_type=F32)
        for c in range(ts // CHUNK):
            rows = slice(c * CHUNK, (c + 1) * CHUNK)
            cos = cos_ref[rows, :]
            sin = sin_ref[rows, :]
            q1 = u_ref[rows, 0:half]
            q2 = u_ref[rows, half:QK_HEAD]
            k1 = u_ref[rows, QK_HEAD:QK_HEAD + half]
            k2 = u_ref[rows, QK_HEAD + half:2 * QK_HEAD]
            qr1 = q1 * cos - q2 * sin
            qr2 = q2 * cos + q1 * sin
            kr1 = (k1 * cos - k2 * sin) * (QK_HEAD ** -0.5)
            kr2 = (k2 * cos + k1 * sin) * (QK_HEAD ** -0.5)
            qdec = qdec_ref[hd]
            kdec = kdec_ref[hd]
            q = jnp.concatenate([qr1, qr2], axis=-1).astype(BF16)
            qd = jnp.concatenate([qr1 * qdec, qr2 * qdec], axis=-1).astype(BF16)
            k = jnp.concatenate([kr1, kr2], axis=-1).astype(BF16)
            kd = jnp.concatenate([kr1 * kdec, kr2 * kdec], axis=-1).astype(BF16)
            v = u_ref[rows, 2 * QK_HEAD:2 * QK_HEAD + V_HEAD].astype(BF16)
            s = lax.dot_general(q, k, (((1,), (1,)), ((), ())), preferred_element_type=F32)
            s = (s * dmask_ref[hd]).astype(BF16)
            inner = jnp.dot(s, v, preferred_element_type=F32)
            st = state_ref[hd]
            cross = jnp.dot(qd, st.astype(BF16), preferred_element_type=F32)
            state_ref[hd] = st * cdec_ref[hd] + lax.dot_general(
                kd, v, (((0,), (0,)), ((), ())), preferred_element_type=F32)
            o = inner + cross
            mu = jnp.mean(o, axis=-1, keepdims=True)
            oc = o - mu
            var = jnp.mean(oc * oc, axis=-1, keepdims=True)
            gn = oc * lax.rsqrt(var + GN_EPS)
            gate = u_ref[rows, 2 * QK_HEAD + V_HEAD:HEAD_COLS]
            y_ref[rows, hd * V_HEAD:(hd + 1) * V_HEAD] = (gn * (gate * _sigmoid(gate))).astype(BF16)

    o = jnp.dot(y_ref[...], wout_ref[...], preferred_element_type=F32)
    o_ref[0] = x_ref[0] + o * _rms_scale(o) * gpost_ref[...]


def _const_spec(shape):
    nd = len(shape)
    return pl.BlockSpec(shape, lambda b, t: (0,) * nd, pipeline_mode=pl.Buffered(1))


def _lru_layer(x, gpre, gpost, w_in, conv_w, conv_b, wg, b_a, b_x, lam, w_out):
    bsz, s, _ = x.shape
    ts = LRU_TS
    x_spec = pl.BlockSpec((1, ts, D_MODEL), lambda b, t: (b, t, 0))
    return pl.pallas_call(
        _lru_kernel,
        grid=(bsz, s // ts),
        in_specs=[x_spec, _const_spec((1, D_MODEL)), _const_spec((1, D_MODEL)),
                  _const_spec((D_MODEL, 2 * D_RNN)), _const_spec((CONV_W, D_RNN)),
                  _const_spec((1, D_RNN)), _const_spec((N_GATE_GROUPS, GATE_GROUP_W, 2 * GATE_GROUP_W)),
                  _const_spec((1, D_RNN)), _const_spec((1, D_RNN)), _const_spec((1, D_RNN)),
                  _const_spec((D_RNN, D_MODEL))],
        out_specs=x_spec,
        out_shape=jax.ShapeDtypeStruct(x.shape, x.dtype),
        scratch_shapes=[
            pltpu.VMEM((ts + SUBLANES, D_RNN), F32),
            pltpu.VMEM((ts, D_RNN), F32),
            pltpu.VMEM((ts, D_RNN), F32),
            pltpu.VMEM((ts, D_RNN), BF16),
            pltpu.VMEM((ts, 2 * D_RNN), F32),
            pltpu.VMEM((ts, D_RNN), BF16),
            pltpu.VMEM((SUBLANES, D_RNN), F32),
        ],
        compiler_params=pltpu.CompilerParams(
            dimension_semantics=("arbitrary", "arbitrary"), vmem_limit_bytes=VMEM_LIMIT),
        name="lru_layer",
    )(x, gpre, gpost, w_in, conv_w, conv_b, wg, b_a, b_x, lam, w_out)


def _ret_layer(x, gpre, gpost, w_in_heads, w_out, cos, sin, dmask, qdec, kdec, cdec):
    bsz, s, _ = x.shape
    ts = RET_TS
    x_spec = pl.BlockSpec((1, ts, D_MODEL), lambda b, t: (b, t, 0))
    rope_spec = pl.BlockSpec((ts, QK_HEAD // 2), lambda b, t: (t, 0))
    return pl.pallas_call(
        _ret_kernel,
        grid=(bsz, s // ts),
        in_specs=[x_spec, _const_spec((1, D_MODEL)), _const_spec((1, D_MODEL)),
                  _const_spec((RET_HEADS, D_MODEL, HEAD_COLS)), _const_spec((V_DIM, D_MODEL)),
                  rope_spec, rope_spec,
                  _const_spec((RET_HEADS, CHUNK, CHUNK)), _const_spec((RET_HEADS, CHUNK, QK_HEAD // 2)),
                  _const_spec((RET_HEADS, CHUNK, QK_HEAD // 2)),
                  pl.BlockSpec(memory_space=pltpu.SMEM)],
        out_specs=x_spec,
        out_shape=jax.ShapeDtypeStruct(x.shape, x.dtype),
        scratch_shapes=[
            pltpu.VMEM((ts, D_MODEL), BF16),
            pltpu.VMEM((ts, HEAD_COLS), F32),
            pltpu.VMEM((ts, V_DIM), BF16),
            pltpu.VMEM((RET_HEADS, QK_HEAD, V_HEAD), F32),
        ],
        compiler_params=pltpu.CompilerParams(
            dimension_semantics=("arbitrary", "arbitrary"), vmem_limit_bytes=VMEM_LIMIT),
        name="ret_layer",
    )(x, gpre, gpost, w_in_heads, w_out, cos, sin, dmask, qdec, kdec, cdec)


def _gate_group_weights(w_a, w_x):
    def bd(w):
        w = w.reshape(N_GATE_GROUPS, GATE_GROUP_BLOCKS, BLOCK_W, BLOCK_W)
        eye = jnp.eye(GATE_GROUP_BLOCKS, dtype=w.dtype)
        full = w[:, :, :, None, :] * eye[None, :, None, :, None]
        return full.reshape(N_GATE_GROUPS, GATE_GROUP_W, GATE_GROUP_W)
    return jnp.concatenate([bd(w_a), bd(w_x)], axis=-1).astype(BF16)


def _ret_head_weights(w_in):
    q = w_in[:, :QK_DIM].reshape(D_MODEL, RET_HEADS, QK_HEAD)
    k = w_in[:, QK_DIM:2 * QK_DIM].reshape(D_MODEL, RET_HEADS, QK_HEAD)
    v = w_in[:, 2 * QK_DIM:2 * QK_DIM + V_DIM].reshape(D_MODEL, RET_HEADS, V_HEAD)
    g = w_in[:, 2 * QK_DIM + V_DIM:].reshape(D_MODEL, RET_HEADS, V_HEAD)
    return jnp.concatenate([q, k, v, g], axis=-1).transpose(1, 0, 2).astype(BF16)


def _retention_tables(s):
    pos = jnp.arange(s, dtype=F32)
    inv_freq = ROPE_BASE ** (-jnp.arange(0, QK_HEAD, 2, dtype=F32) / QK_HEAD)
    ang = pos[:, None] * inv_freq[None, :]
    cos, sin = jnp.cos(ang), jnp.sin(ang)
    log_g = jnp.log(1.0 - jnp.exp2(-5.0 - jnp.arange(RET_HEADS, dtype=F32)))
    idx = jnp.arange(CHUNK, dtype=F32)
    diff = idx[:, None] - idx[None, :]
    dmask = jnp.where(diff[None] >= 0, jnp.exp(jnp.maximum(diff, 0.0)[None] * log_g[:, None, None]), 0.0)
    q_decay = jnp.exp((idx[None, :] + 1.0) * log_g[:, None])[..., None]
    k_decay = jnp.exp((CHUNK - 1.0 - idx[None, :]) * log_g[:, None])[..., None]
    qdec = jnp.broadcast_to(q_decay, (RET_HEADS, CHUNK, QK_HEAD // 2))
    kdec = jnp.broadcast_to(k_decay, (RET_HEADS, CHUNK, QK_HEAD // 2))
    cdec = jnp.exp(CHUNK * log_g)
    return cos, sin, dmask, qdec, kdec, cdec


def kernel(x, norm_pre, norm_post, lru_w_in, lru_conv_w, lru_conv_b, lru_w_a, lru_b_a, lru_w_x, lru_b_x,
           lru_lambda, lru_w_out, ret_w_in, ret_w_out):
    s = x.shape[1]
    tables = _retention_tables(s)
    for layer in range(DEPTH):
        j = layer // 2
        gpre = norm_pre[layer][None, :]
        gpost = norm_post[layer][None, :]
        if layer % 2 == 0:
            x = _lru_layer(x, gpre, gpost, lru_w_in[j].astype(BF16), lru_conv_w[j], lru_conv_b[j][None, :],
                           _gate_group_weights(lru_w_a[j], lru_w_x[j]), lru_b_a[j][None, :],
                           lru_b_x[j][None, :], lru_lambda[j][None, :], lru_w_out[j].astype(BF16))
        else:
            x = _ret_layer(x, gpre, gpost, _ret_head_weights(ret_w_in[j]), ret_w_out[j].astype(BF16), *tables)
    return x
```

```python
import jax
import jax.numpy as jnp
from jax import lax
from jax.experimental import pallas as pl
from jax.experimental.pallas import tpu as pltpu

D_MODEL = 1024
DEPTH = 4
D_RNN = 1536
N_GATE_BLOCKS = 16
BLOCK_W = 96
CONV_W = 4
C_RG = 8.0
RET_HEADS = 4
QK_HEAD = 256
V_HEAD = 512
QK_DIM = RET_HEADS * QK_HEAD
V_DIM = RET_HEADS * V_HEAD
CHUNK = 128
ROPE_BASE = 10000.0
RMS_EPS = 1e-6
GN_EPS = 1e-5

GATE_GROUP_BLOCKS = 4
GATE_GROUP_W = GATE_GROUP_BLOCKS * BLOCK_W
N_GATE_GROUPS = N_GATE_BLOCKS // GATE_GROUP_BLOCKS
HEAD_COLS = 2 * QK_HEAD + 2 * V_HEAD
SUBLANES = 8

LRU_TS = 512
LRU_P = LRU_TS // SUBLANES
LRU_BLK = 128
LRU_RG = LRU_BLK // SUBLANES
LRU_NB = LRU_TS // LRU_BLK
RET_TS = 512
VMEM_LIMIT = 56 * 1024 * 1024

F32 = jnp.float32
BF16 = jnp.bfloat16
LOG2_E = 1.4426950408889634
SQRT_TINY = 1e-30


def _sigmoid(v):
    return 0.5 * jnp.tanh(0.5 * v) + 0.5


def _rms_scale(v):
    return lax.rsqrt(jnp.mean(v * v, axis=-1, keepdims=True) + RMS_EPS)


def _shift_rows(v, d, fill, rowid):
    return jnp.where(rowid >= d, pltpu.roll(v, d, axis=0), fill)


def _scan8(a, b, rowid):
    for d in (1, 2, 4):
        b = a * _shift_rows(b, d, 0.0, rowid) + b
        a = a * _shift_rows(a, d, 1.0, rowid)
    return a, b


def _x_copies(x_hbm, xbuf, sem, b, t, slot):
    return [pltpu.make_async_copy(x_hbm.at[b, pl.ds(t * LRU_TS + s * LRU_P, LRU_P), :],
                                  xbuf.at[slot, :, s, :], sem.at[slot]) for s in range(SUBLANES)]


def _o_copies(obuf, o_hbm, sem, b, t, slot):
    return [pltpu.make_async_copy(obuf.at[slot, :, s, :],
                                  o_hbm.at[b, pl.ds(t * LRU_TS + s * LRU_P, LRU_P), :], sem.at[slot])
            for s in range(SUBLANES)]


def _lru_kernel(x_hbm, gpre_ref, gpost_ref, win_hbm, convw_ref, convb_ref, wg_hbm, hba_ref, hbx_ref,
                lam_ref, wout_hbm, o_hbm, xbuf, obuf, in_sem, out_sem, w_sem, win_ref, wg_ref, wout_ref,
                h16_ref, xp_ref, gate_ref, xb_ref, xb16_ref, g_ref, y_ref, tail_ref, hc_ref, cin_ref,
                c2_ref, carry_ref):
    nb_t = pl.num_programs(1)
    b = pl.program_id(0)
    t = pl.program_id(1)
    n = b * nb_t + t
    n_steps = pl.num_programs(0) * nb_t
    slot = n % 2
    nslot = 1 - slot

    @pl.when(n == 0)
    def _():
        for cp in _x_copies(x_hbm, xbuf, in_sem, b, t, slot):
            cp.start()
        w_copies = [pltpu.make_async_copy(src, dst, w_sem.at[i]) for i, (src, dst) in enumerate(
            ((win_hbm, win_ref), (wg_hbm, wg_ref), (wout_hbm, wout_ref)))]
        for cp in w_copies:
            cp.start()
        for cp in w_copies:
            cp.wait()

    @pl.when(n + 1 < n_steps)
    def _():
        last_t = t + 1 == nb_t
        for cp in _x_copies(x_hbm, xbuf, in_sem, jnp.where(last_t, b + 1, b), jnp.where(last_t, 0, t + 1),
                            nslot):
            cp.start()

    @pl.when(n >= 2)
    def _():
        for cp in _o_copies(obuf, o_hbm, out_sem, b, t, slot):
            cp.wait()

    @pl.when(t == 0)
    def _():
        tail_ref[...] = jnp.zeros_like(tail_ref)
        hc_ref[...] = jnp.zeros_like(hc_ref)

    for cp in _x_copies(x_hbm, xbuf, in_sem, b, t, slot):
        cp.wait()

    lam = lam_ref[...]
    c2_ref[...] = (-0.5 * C_RG * LOG2_E) * (jnp.maximum(-lam, 0.0) + jnp.log1p(jnp.exp(-jnp.abs(lam))))
    rowid = lax.broadcasted_iota(jnp.int32, (SUBLANES, GATE_GROUP_W), 0)
    rowid_full = lax.broadcasted_iota(jnp.int32, (SUBLANES, D_RNN), 0)
    hdr = (CONV_W - 1) * SUBLANES

    def norm_in(k):
        for ip in range(LRU_RG // 2):
            i0 = k * LRU_RG + 2 * ip
            hs = []
            for d in range(2):
                v = xbuf[slot, i0 + d]
                hs.append(v * _rms_scale(v) * gpre_ref[...])
            h16_ref[SUBLANES * i0:SUBLANES * (i0 + 2), :] = jnp.concatenate(hs, axis=0).astype(BF16)

    def proj_in(k):
        rows = slice(LRU_BLK * k, LRU_BLK * (k + 1))
        hblk = h16_ref[rows, :]
        xp_ref[hdr + LRU_BLK * k:hdr + LRU_BLK * (k + 1), :] = jnp.dot(
            hblk, win_ref[:, :D_RNN], preferred_element_type=F32)
        gate_ref[rows, :] = jnp.dot(hblk, win_ref[:, D_RNN:], preferred_element_type=F32)

    def conv_header():
        for g in range(CONV_W - 1):
            rows = slice(SUBLANES * g, SUBLANES * (g + 1))
            src = hdr + SUBLANES * (LRU_P - (CONV_W - 1) + g)
            cur = xp_ref[src:src + SUBLANES, :]
            merged = jnp.where(rowid_full == SUBLANES - 1, tail_ref[rows, :], cur)
            xp_ref[rows, :] = pltpu.roll(merged, 1, axis=0)
            tail_ref[rows, :] = cur

    def conv(k):
        for ip in range(LRU_RG // 2):
            i0 = k * LRU_RG + 2 * ip
            outs = []
            for d in range(2):
                r0 = SUBLANES * (i0 + d)
                acc = convb_ref[...]
                for j in range(CONV_W):
                    acc = acc + convw_ref[SUBLANES * j:SUBLANES * (j + 1), :] * xp_ref[
                        r0 + SUBLANES * j:r0 + SUBLANES * (j + 1), :]
                xb_ref[r0:r0 + SUBLANES, :] = acc
                outs.append(acc)
            xb16_ref[SUBLANES * i0:SUBLANES * (i0 + 2), :] = jnp.concatenate(outs, axis=0).astype(BF16)

    def gates(k):
        rows = slice(LRU_BLK * k, LRU_BLK * (k + 1))
        for g in range(N_GATE_GROUPS):
            g_ref[rows, 2 * GATE_GROUP_W * g:2 * GATE_GROUP_W * (g + 1)] = jnp.dot(
                xb16_ref[rows, GATE_GROUP_W * g:GATE_GROUP_W * (g + 1)], wg_ref[g],
                preferred_element_type=F32)

    def local_scan(k):
        for g in range(N_GATE_GROUPS):
            cs = slice(GATE_GROUP_W * g, GATE_GROUP_W * (g + 1))
            ca = slice(2 * GATE_GROUP_W * g, 2 * GATE_GROUP_W * g + GATE_GROUP_W)
            cx = slice(2 * GATE_GROUP_W * g + GATE_GROUP_W, 2 * GATE_GROUP_W * (g + 1))
            if k == 0:
                hl = jnp.zeros((SUBLANES, GATE_GROUP_W), F32)
                ac = jnp.ones((SUBLANES, GATE_GROUP_W), F32)
            else:
                hl = carry_ref[0, :, cs]
                ac = carry_ref[1, :, cs]
            c2 = c2_ref[:, cs]
            hba = hba_ref[:, cs]
            hbx = hbx_ref[:, cs]
            for i in range(LRU_RG):
                rows = slice(SUBLANES * (k * LRU_RG + i), SUBLANES * (k * LRU_RG + i + 1))
                tr = jnp.tanh(g_ref[rows, ca] + hba)
                ti = jnp.tanh(g_ref[rows, cx] + hbx)
                a = jnp.exp2(c2 * tr + c2)
                z = 1.0 - a * a
                bt = (z * lax.rsqrt(jnp.maximum(z, SQRT_TINY))) * ((ti + 1.0) * xb_ref[rows, cs])
                hl = a * hl + bt
                ac = a * ac
                g_ref[rows, ca] = hl
                g_ref[rows, cx] = ac
            if k + 1 < LRU_NB:
                carry_ref[0, :, cs] = hl
                carry_ref[1, :, cs] = ac

    def segment_carries():
        last = slice(LRU_TS - SUBLANES, LRU_TS)
        for g in range(N_GATE_GROUPS):
            cs = slice(GATE_GROUP_W * g, GATE_GROUP_W * (g + 1))
            ca = slice(2 * GATE_GROUP_W * g, 2 * GATE_GROUP_W * g + GATE_GROUP_W)
            cx = slice(2 * GATE_GROUP_W * g + GATE_GROUP_W, 2 * GATE_GROUP_W * (g + 1))
            hprev = hc_ref[:, cs]
            acum, bcum = _scan8(g_ref[last, cx], g_ref[last, ca], rowid)
            hend = acum * hprev + bcum
            cin_ref[:, cs] = jnp.where(rowid == 0, hprev, pltpu.roll(hend, 1, axis=0))
            hc_ref[:, cs] = jnp.broadcast_to(hend[SUBLANES - 1:SUBLANES, :], hend.shape)

    def gate_out(k):
        for g in range(N_GATE_GROUPS):
            cs = slice(GATE_GROUP_W * g, GATE_GROUP_W * (g + 1))
            ca = slice(2 * GATE_GROUP_W * g, 2 * GATE_GROUP_W * g + GATE_GROUP_W)
            cx = slice(2 * GATE_GROUP_W * g + GATE_GROUP_W, 2 * GATE_GROUP_W * (g + 1))
            cin = cin_ref[:, cs]
            for ip in range(LRU_RG // 2):
                ys = []
                for d in range(2):
                    r0 = SUBLANES * (k * LRU_RG + 2 * ip + d)
                    rows = slice(r0, r0 + SUBLANES)
                    hseq = g_ref[rows, ca] + g_ref[rows, cx] * cin
                    hg = gate_ref[rows, cs]
                    ys.append(hseq * (hg * jnp.tanh(hg) + hg))
                r0 = SUBLANES * (k * LRU_RG + 2 * ip)
                y_ref[r0:r0 + 2 * SUBLANES, cs] = jnp.concatenate(ys, axis=0).astype(BF16)

    def proj_out(k):
        o = jnp.dot(y_ref[LRU_BLK * k:LRU_BLK * (k + 1), :], wout_ref[...], preferred_element_type=F32)
        for i in range(LRU_RG):
            ov = o[SUBLANES * i:SUBLANES * (i + 1)]
            obuf[slot, k * LRU_RG + i] = xbuf[slot, k * LRU_RG + i] + ov * _rms_scale(ov) * gpost_ref[...]

    norm_in(LRU_NB - 1)
    proj_in(LRU_NB - 1)
    norm_in(0)
    proj_in(0)
    conv_header()
    for k in range(LRU_NB):
        if k + 1 < LRU_NB - 1:
            norm_in(k + 1)
            proj_in(k + 1)
        conv(k)
        gates(k)
        local_scan(k)
    segment_carries()
    for k in range(LRU_NB):
        gate_out(k)
        proj_out(k)

    for cp in _o_copies(obuf, o_hbm, out_sem, b, t, slot):
        cp.start()

    @pl.when(n == n_steps - 1)
    def _():
        for cp in _o_copies(obuf, o_hbm, out_sem, b, t, slot):
            cp.wait()
        for cp in _o_copies(obuf, o_hbm, out_sem, b, t, nslot):
            cp.wait()


def _ret_kernel(x_ref, gpre_ref, gpost_ref, win_hbm, wout_hbm, cos_ref, sin_ref, dmask_ref, qdec_ref,
                kdec_ref, cdec_ref, o_ref, w_sem, win_ref, wout_ref, h_ref, u_ref, y_ref, state_ref):
    ts = x_ref.shape[1]
    t = pl.program_id(1)

    @pl.when((pl.program_id(0) == 0) & (t == 0))
    def _():
        w_copies = [pltpu.make_async_copy(win_hbm, win_ref, w_sem.at[0]),
                    pltpu.make_async_copy(wout_hbm, wout_ref, w_sem.at[1])]
        for cp in w_copies:
            cp.start()
        for cp in w_copies:
            cp.wait()

    @pl.when(t == 0)
    def _():
        state_ref[...] = jnp.zeros_like(state_ref)

    x = x_ref[0]
    h_ref[...] = (x * _rms_scale(x) * gpre_ref[...]).astype(BF16)
    half = QK_HEAD // 2
    for hd in range(RET_HEADS):
        u_ref[...] = jnp.dot(h_ref[...], win_ref[hd], preferred_element_type=F32)
        for c in range(ts // CHUNK):
            rows = slice(c * CHUNK, (c + 1) * CHUNK)
            cos = cos_ref[rows, :]
            sin = sin_ref[rows, :]
            q1 = u_ref[rows, 0:half]
            q2 = u_ref[rows, half:QK_HEAD]
            k1 = u_ref[rows, QK_HEAD:QK_HEAD + half]
            k2 = u_ref[rows, QK_HEAD + half:2 * QK_HEAD]
            qr1 = q1 * cos - q2 * sin
            qr2 = q2 * cos + q1 * sin
            kr1 = (k1 * cos - k2 * sin) * (QK_HEAD ** -0.5)
            kr2 = (k2 * cos + k1 * sin) * (QK_HEAD ** -0.5)
            qdec = qdec_ref[hd]
            kdec = kdec_ref[hd]
            q = jnp.concatenate([qr1, qr2], axis=-1).astype(BF16)
            qd = jnp.concatenate([qr1 * qdec, qr2 * qdec], axis=-1).astype(BF16)
            k = jnp.concatenate([kr1, kr2], axis=-1).astype(BF16)
            kd = jnp.concatenate([kr1 * kdec, kr2 * kdec], axis=-1).astype(BF16)
            v = u_ref[rows, 2 * QK_HEAD:2 * QK_HEAD + V_HEAD].astype(BF16)
            s = lax.dot_general(q, k, (((1,), (1,)), ((), ())), preferred_element_type=F32)
            s = (s * dmask_ref[hd]).astype(BF16)
            inner = jnp.dot(s, v, preferred_element_type=F32)
            st = state_ref[hd]
            cross = jnp.dot(qd, st.astype(BF16), preferred_element_type=F32)
            state_ref[hd] = st * cdec_ref[hd] + lax.dot_general(
                kd, v, (((0,), (0,)), ((), ())), preferred_element_type=F32)
            o = inner + cross
            mu = jnp.mean(o, axis=-1, keepdims=True)
            oc = o - mu
            var = jnp.mean(oc * oc, axis=-1, keepdims=True)
            gn = oc * lax.rsqrt(var + GN_EPS)
            gate = u_ref[rows, 2 * QK_HEAD + V_HEAD:HEAD_COLS]
            y_ref[rows, hd * V_HEAD:(hd + 1) * V_HEAD] = (gn * (gate * _sigmoid(gate))).astype(BF16)

    o = jnp.dot(y_ref[...], wout_ref[...], preferred_element_type=F32)
    o_ref[0] = x_ref[0] + o * _rms_scale(o) * gpost_ref[...]


def _const_spec(shape):
    nd = len(shape)
    return pl.BlockSpec(shape, lambda b, t: (0,) * nd, pipeline_mode=pl.Buffered(1))


def _rows8(v):
    return jnp.broadcast_to(v[None, :], (SUBLANES, v.shape[0]))


def _lru_layer(x, gpre, gpost, w_in, conv_w, conv_b, wg, b_a, b_x, lam, w_out):
    bsz, s, _ = x.shape
    any_spec = pl.BlockSpec(memory_space=pl.ANY)
    return pl.pallas_call(
        _lru_kernel,
        grid=(bsz, s // LRU_TS),
        in_specs=[any_spec, _const_spec((SUBLANES, D_MODEL)), _const_spec((SUBLANES, D_MODEL)),
                  any_spec, _const_spec((CONV_W * SUBLANES, D_RNN)), _const_spec((SUBLANES, D_RNN)),
                  any_spec, _const_spec((SUBLANES, D_RNN)), _const_spec((SUBLANES, D_RNN)),
                  _const_spec((SUBLANES, D_RNN)), any_spec],
        out_specs=any_spec,
        out_shape=jax.ShapeDtypeStruct(x.shape, x.dtype),
        scratch_shapes=[
            pltpu.VMEM((2, LRU_P, SUBLANES, D_MODEL), F32),
            pltpu.VMEM((2, LRU_P, SUBLANES, D_MODEL), F32),
            pltpu.SemaphoreType.DMA((2,)),
            pltpu.SemaphoreType.DMA((2,)),
            pltpu.SemaphoreType.DMA((3,)),
            pltpu.VMEM((D_MODEL, 2 * D_RNN), BF16),
            pltpu.VMEM((N_GATE_GROUPS, GATE_GROUP_W, 2 * GATE_GROUP_W), BF16),
            pltpu.VMEM((D_RNN, D_MODEL), BF16),
            pltpu.VMEM((LRU_TS, D_MODEL), BF16),
            pltpu.VMEM((LRU_TS + (CONV_W - 1) * SUBLANES, D_RNN), F32),
            pltpu.VMEM((LRU_TS, D_RNN), F32),
            pltpu.VMEM((LRU_TS, D_RNN), F32),
            pltpu.VMEM((LRU_TS, D_RNN), BF16),
            pltpu.VMEM((LRU_TS, 2 * D_RNN), F32),
            pltpu.VMEM((LRU_TS, D_RNN), BF16),
            pltpu.VMEM(((CONV_W - 1) * SUBLANES, D_RNN), F32),
            pltpu.VMEM((SUBLANES, D_RNN), F32),
            pltpu.VMEM((SUBLANES, D_RNN), F32),
            pltpu.VMEM((SUBLANES, D_RNN), F32),
            pltpu.VMEM((2, SUBLANES, D_RNN), F32),
        ],
        compiler_params=pltpu.CompilerParams(
            dimension_semantics=("arbitrary", "arbitrary"), vmem_limit_bytes=VMEM_LIMIT),
        name="lru_layer",
    )(x, _rows8(gpre), _rows8(gpost), w_in, jnp.repeat(0.5 * conv_w, SUBLANES, axis=0), _rows8(0.5 * conv_b),
      wg, _rows8(0.5 * b_a), _rows8(0.5 * b_x), _rows8(lam), w_out)


def _ret_layer(x, gpre, gpost, w_in_heads, w_out, cos, sin, dmask, qdec, kdec, cdec):
    bsz, s, _ = x.shape
    ts = RET_TS
    x_spec = pl.BlockSpec((1, ts, D_MODEL), lambda b, t: (b, t, 0))
    rope_spec = pl.BlockSpec((ts, QK_HEAD // 2), lambda b, t: (t, 0))
    return pl.pallas_call(
        _ret_kernel,
        grid=(bsz, s // ts),
        in_specs=[x_spec, _const_spec((1, D_MODEL)), _const_spec((1, D_MODEL)),
                  pl.BlockSpec(memory_space=pl.ANY), pl.BlockSpec(memory_space=pl.ANY),
                  rope_spec, rope_spec,
                  _const_spec((RET_HEADS, CHUNK, CHUNK)), _const_spec((RET_HEADS, CHUNK, QK_HEAD // 2)),
                  _const_spec((RET_HEADS, CHUNK, QK_HEAD // 2)),
                  pl.BlockSpec(memory_space=pltpu.SMEM)],
        out_specs=x_spec,
        out_shape=jax.ShapeDtypeStruct(x.shape, x.dtype),
        scratch_shapes=[
            pltpu.SemaphoreType.DMA((2,)),
            pltpu.VMEM((RET_HEADS, D_MODEL, HEAD_COLS), BF16),
            pltpu.VMEM((V_DIM, D_MODEL), BF16),
            pltpu.VMEM((ts, D_MODEL), BF16),
            pltpu.VMEM((ts, HEAD_COLS), F32),
            pltpu.VMEM((ts, V_DIM), BF16),
            pltpu.VMEM((RET_HEADS, QK_HEAD, V_HEAD), F32),
        ],
        compiler_params=pltpu.CompilerParams(
            dimension_semantics=("arbitrary", "arbitrary"), vmem_limit_bytes=VMEM_LIMIT),
        name="ret_layer",
    )(x, gpre, gpost, w_in_heads, w_out, cos, sin, dmask, qdec, kdec, cdec)


def _gate_group_weights(w_a, w_x):
    def bd(w):
        w = w.reshape(N_GATE_GROUPS, GATE_GROUP_BLOCKS, BLOCK_W, BLOCK_W)
        eye = jnp.eye(GATE_GROUP_BLOCKS, dtype=w.dtype)
        full = w[:, :, :, None, :] * eye[None, :, None, :, None]
        return full.reshape(N_GATE_GROUPS, GATE_GROUP_W, GATE_GROUP_W)
    return jnp.concatenate([bd(w_a), bd(w_x)], axis=-1).astype(BF16)


def _ret_head_weights(w_in):
    q = w_in[:, :QK_DIM].reshape(D_MODEL, RET_HEADS, QK_HEAD)
    k = w_in[:, QK_DIM:2 * QK_DIM].reshape(D_MODEL, RET_HEADS, QK_HEAD)
    v = w_in[:, 2 * QK_DIM:2 * QK_DIM + V_DIM].reshape(D_MODEL, RET_HEADS, V_HEAD)
    g = w_in[:, 2 * QK_DIM + V_DIM:].reshape(D_MODEL, RET_HEADS, V_HEAD)
    return jnp.concatenate([q, k, v, g], axis=-1).transpose(1, 0, 2).astype(BF16)


def _retention_tables(s):
    pos = jnp.arange(s, dtype=F32)
    inv_freq = ROPE_BASE ** (-jnp.arange(0, QK_HEAD, 2, dtype=F32) / QK_HEAD)
    ang = pos[:, None] * inv_freq[None, :]
    cos, sin = jnp.cos(ang), jnp.sin(ang)
    log_g = jnp.log(1.0 - jnp.exp2(-5.0 - jnp.arange(RET_HEADS, dtype=F32)))
    idx = jnp.arange(CHUNK, dtype=F32)
    diff = idx[:, None] - idx[None, :]
    dmask = jnp.where(diff[None] >= 0, jnp.exp(jnp.maximum(diff, 0.0)[None] * log_g[:, None, None]), 0.0)
    q_decay = jnp.exp((idx[None, :] + 1.0) * log_g[:, None])[..., None]
    k_decay = jnp.exp((CHUNK - 1.0 - idx[None, :]) * log_g[:, None])[..., None]
    qdec = jnp.broadcast_to(q_decay, (RET_HEADS, CHUNK, QK_HEAD // 2))
    kdec = jnp.broadcast_to(k_decay, (RET_HEADS, CHUNK, QK_HEAD // 2))
    cdec = jnp.exp(CHUNK * log_g)
    return cos, sin, dmask, qdec, kdec, cdec


def kernel(x, norm_pre, norm_post, lru_w_in, lru_conv_w, lru_conv_b, lru_w_a, lru_b_a, lru_w_x, lru_b_x,
           lru_lambda, lru_w_out, ret_w_in, ret_w_out):
    s = x.shape[1]
    tables = _retention_tables(s)
    for layer in range(DEPTH):
        j = layer // 2
        if layer % 2 == 0:
            w_in = jnp.concatenate([lru_w_in[j][:, :D_RNN], 0.5 * lru_w_in[j][:, D_RNN:]], axis=1).astype(BF16)
            x = _lru_layer(x, norm_pre[layer], norm_post[layer], w_in, lru_conv_w[j],
                           lru_conv_b[j], _gate_group_weights(lru_w_a[j], lru_w_x[j]), lru_b_a[j],
                           lru_b_x[j], lru_lambda[j], lru_w_out[j].astype(BF16))
        else:
            x = _ret_layer(x, norm_pre[layer][None, :], norm_post[layer][None, :],
                           _ret_head_weights(ret_w_in[j]), ret_w_out[j].astype(BF16), *tables)
    return x
```

```python
import jax
import jax.numpy as jnp
from jax import lax
from jax.experimental import pallas as pl
from jax.experimental.pallas import tpu as pltpu

D_MODEL = 1024
DEPTH = 4
D_RNN = 1536
N_GATE_BLOCKS = 16
BLOCK_W = 96
CONV_W = 4
C_RG = 8.0
RET_HEADS = 4
QK_HEAD = 256
V_HEAD = 512
QK_DIM = RET_HEADS * QK_HEAD
V_DIM = RET_HEADS * V_HEAD
CHUNK = 256
ROPE_BASE = 10000.0
RMS_EPS = 1e-6
GN_EPS = 1e-5

GATE_GROUP_BLOCKS = 4
GATE_GROUP_W = GATE_GROUP_BLOCKS * BLOCK_W
N_GATE_GROUPS = N_GATE_BLOCKS // GATE_GROUP_BLOCKS
HEAD_COLS = 2 * QK_HEAD + 2 * V_HEAD
SUBLANES = 8

LRU_TS = 512
LRU_P = LRU_TS // SUBLANES
LRU_BLK = 128
LRU_RG = LRU_BLK // SUBLANES
LRU_NB = LRU_TS // LRU_BLK
RET_TS = 512
VMEM_LIMIT = 56 * 1024 * 1024

F32 = jnp.float32
BF16 = jnp.bfloat16
LOG2_E = 1.4426950408889634
SQRT_TINY = 1e-30


def _sigmoid(v):
    return 0.5 * jnp.tanh(0.5 * v) + 0.5


def _rms_scale(v):
    return lax.rsqrt(jnp.mean(v * v, axis=-1, keepdims=True) + RMS_EPS)


def _shift_rows(v, d, fill, rowid):
    return jnp.where(rowid >= d, pltpu.roll(v, d, axis=0), fill)


def _scan8(a, b, rowid):
    for d in (1, 2, 4):
        b = a * _shift_rows(b, d, 0.0, rowid) + b
        a = a * _shift_rows(a, d, 1.0, rowid)
    return a, b


def _x_copies(x_hbm, xbuf, sem, b, t, slot):
    return [pltpu.make_async_copy(x_hbm.at[b, pl.ds(t * LRU_TS + s * LRU_P, LRU_P), :],
                                  xbuf.at[slot, :, s, :], sem.at[slot]) for s in range(SUBLANES)]


def _o_copies(obuf, o_hbm, sem, b, t, slot):
    return [pltpu.make_async_copy(obuf.at[slot, :, s, :],
                                  o_hbm.at[b, pl.ds(t * LRU_TS + s * LRU_P, LRU_P), :], sem.at[slot])
            for s in range(SUBLANES)]


def _lru_kernel(x_hbm, gpre_ref, gpost_ref, win_hbm, convw_ref, convb_ref, wg_hbm, hba_ref, hbx_ref,
                lam_ref, wout_hbm, o_hbm, xbuf, obuf, in_sem, out_sem, w_sem, win_ref, wg_ref, wout_ref,
                h16_ref, xp_ref, gate_ref, xb_ref, xb16_ref, g_ref, y_ref, tail_ref, hc_ref, cin_ref,
                c2_ref, carry_ref):
    nb_t = pl.num_programs(1)
    b = pl.program_id(0)
    t = pl.program_id(1)
    n = b * nb_t + t
    n_steps = pl.num_programs(0) * nb_t
    slot = n % 2
    nslot = 1 - slot

    @pl.when(n == 0)
    def _():
        for cp in _x_copies(x_hbm, xbuf, in_sem, b, t, slot):
            cp.start()
        w_copies = [pltpu.make_async_copy(src, dst, w_sem.at[i]) for i, (src, dst) in enumerate(
            ((win_hbm, win_ref), (wg_hbm, wg_ref), (wout_hbm, wout_ref)))]
        for cp in w_copies:
            cp.start()
        for cp in w_copies:
            cp.wait()

    @pl.when(n + 1 < n_steps)
    def _():
        last_t = t + 1 == nb_t
        for cp in _x_copies(x_hbm, xbuf, in_sem, jnp.where(last_t, b + 1, b), jnp.where(last_t, 0, t + 1),
                            nslot):
            cp.start()

    @pl.when(n >= 2)
    def _():
        for cp in _o_copies(obuf, o_hbm, out_sem, b, t, slot):
            cp.wait()

    @pl.when(t == 0)
    def _():
        tail_ref[...] = jnp.zeros_like(tail_ref)
        hc_ref[...] = jnp.zeros_like(hc_ref)

    for cp in _x_copies(x_hbm, xbuf, in_sem, b, t, slot):
        cp.wait()

    lam = lam_ref[...]
    c2_ref[...] = (-0.5 * C_RG * LOG2_E) * (jnp.maximum(-lam, 0.0) + jnp.log1p(jnp.exp(-jnp.abs(lam))))
    rowid = lax.broadcasted_iota(jnp.int32, (SUBLANES, GATE_GROUP_W), 0)
    rowid_full = lax.broadcasted_iota(jnp.int32, (SUBLANES, D_RNN), 0)
    hdr = (CONV_W - 1) * SUBLANES

    def norm_in(k):
        for ip in range(LRU_RG // 2):
            i0 = k * LRU_RG + 2 * ip
            hs = []
            for d in range(2):
                v = xbuf[slot, i0 + d]
                hs.append(v * _rms_scale(v) * gpre_ref[...])
            h16_ref[SUBLANES * i0:SUBLANES * (i0 + 2), :] = jnp.concatenate(hs, axis=0).astype(BF16)

    def proj_in(k):
        rows = slice(LRU_BLK * k, LRU_BLK * (k + 1))
        hblk = h16_ref[rows, :]
        xp_ref[hdr + LRU_BLK * k:hdr + LRU_BLK * (k + 1), :] = jnp.dot(
            hblk, win_ref[:, :D_RNN], preferred_element_type=F32)
        gate_ref[rows, :] = jnp.dot(hblk, win_ref[:, D_RNN:], preferred_element_type=F32)

    def conv_header():
        for g in range(CONV_W - 1):
            rows = slice(SUBLANES * g, SUBLANES * (g + 1))
            src = hdr + SUBLANES * (LRU_P - (CONV_W - 1) + g)
            cur = xp_ref[src:src + SUBLANES, :]
            merged = jnp.where(rowid_full == SUBLANES - 1, tail_ref[rows, :], cur)
            xp_ref[rows, :] = pltpu.roll(merged, 1, axis=0)
            tail_ref[rows, :] = cur

    def conv(k):
        for ip in range(LRU_RG // 2):
            i0 = k * LRU_RG + 2 * ip
            outs = []
            for d in range(2):
                r0 = SUBLANES * (i0 + d)
                acc = convb_ref[...]
                for j in range(CONV_W):
                    acc = acc + convw_ref[SUBLANES * j:SUBLANES * (j + 1), :] * xp_ref[
                        r0 + SUBLANES * j:r0 + SUBLANES * (j + 1), :]
                xb_ref[r0:r0 + SUBLANES, :] = acc
                outs.append(acc)
            xb16_ref[SUBLANES * i0:SUBLANES * (i0 + 2), :] = jnp.concatenate(outs, axis=0).astype(BF16)

    def gates(k):
        rows = slice(LRU_BLK * k, LRU_BLK * (k + 1))
        for g in range(N_GATE_GROUPS):
            g_ref[rows, 2 * GATE_GROUP_W * g:2 * GATE_GROUP_W * (g + 1)] = jnp.dot(
                xb16_ref[rows, GATE_GROUP_W * g:GATE_GROUP_W * (g + 1)], wg_ref[g],
                preferred_element_type=F32)

    def local_scan(k):
        for g in range(N_GATE_GROUPS):
            cs = slice(GATE_GROUP_W * g, GATE_GROUP_W * (g + 1))
            ca = slice(2 * GATE_GROUP_W * g, 2 * GATE_GROUP_W * g + GATE_GROUP_W)
            cx = slice(2 * GATE_GROUP_W * g + GATE_GROUP_W, 2 * GATE_GROUP_W * (g + 1))
            if k == 0:
                hl = jnp.zeros((SUBLANES, GATE_GROUP_W), F32)
                ac = jnp.ones((SUBLANES, GATE_GROUP_W), F32)
            else:
                hl = carry_ref[0, :, cs]
                ac = carry_ref[1, :, cs]
            c2 = c2_ref[:, cs]
            hba = hba_ref[:, cs]
            hbx = hbx_ref[:, cs]
            for i in range(LRU_RG):
                rows = slice(SUBLANES * (k * LRU_RG + i), SUBLANES * (k * LRU_RG + i + 1))
                tr = jnp.tanh(g_ref[rows, ca] + hba)
                ti = jnp.tanh(g_ref[rows, cx] + hbx)
                a = jnp.exp2(c2 * tr + c2)
                z = 1.0 - a * a
                bt = (z * lax.rsqrt(jnp.maximum(z, SQRT_TINY))) * ((ti + 1.0) * xb_ref[rows, cs])
                hl = a * hl + bt
                ac = a * ac
                g_ref[rows, ca] = hl
                g_ref[rows, cx] = ac
            if k + 1 < LRU_NB:
                carry_ref[0, :, cs] = hl
                carry_ref[1, :, cs] = ac

    def segment_carries():
        last = slice(LRU_TS - SUBLANES, LRU_TS)
        for g in range(N_GATE_GROUPS):
            cs = slice(GATE_GROUP_W * g, GATE_GROUP_W * (g + 1))
            ca = slice(2 * GATE_GROUP_W * g, 2 * GATE_GROUP_W * g + GATE_GROUP_W)
            cx = slice(2 * GATE_GROUP_W * g + GATE_GROUP_W, 2 * GATE_GROUP_W * (g + 1))
            hprev = hc_ref[:, cs]
            acum, bcum = _scan8(g_ref[last, cx], g_ref[last, ca], rowid)
            hend = acum * hprev + bcum
            cin_ref[:, cs] = jnp.where(rowid == 0, hprev, pltpu.roll(hend, 1, axis=0))
            hc_ref[:, cs] = jnp.broadcast_to(hend[SUBLANES - 1:SUBLANES, :], hend.shape)

    def gate_out(k):
        for g in range(N_GATE_GROUPS):
            cs = slice(GATE_GROUP_W * g, GATE_GROUP_W * (g + 1))
            ca = slice(2 * GATE_GROUP_W * g, 2 * GATE_GROUP_W * g + GATE_GROUP_W)
            cx = slice(2 * GATE_GROUP_W * g + GATE_GROUP_W, 2 * GATE_GROUP_W * (g + 1))
            cin = cin_ref[:, cs]
            for ip in range(LRU_RG // 2):
                ys = []
                for d in range(2):
                    r0 = SUBLANES * (k * LRU_RG + 2 * ip + d)
                    rows = slice(r0, r0 + SUBLANES)
                    hseq = g_ref[rows, ca] + g_ref[rows, cx] * cin
                    hg = gate_ref[rows, cs]
                    ys.append(hseq * (hg * jnp.tanh(hg) + hg))
                r0 = SUBLANES * (k * LRU_RG + 2 * ip)
                y_ref[r0:r0 + 2 * SUBLANES, cs] = jnp.concatenate(ys, axis=0).astype(BF16)

    def proj_out(k):
        o = jnp.dot(y_ref[LRU_BLK * k:LRU_BLK * (k + 1), :], wout_ref[...], preferred_element_type=F32)
        for i in range(LRU_RG):
            ov = o[SUBLANES * i:SUBLANES * (i + 1)]
            obuf[slot, k * LRU_RG + i] = xbuf[slot, k * LRU_RG + i] + ov * _rms_scale(ov) * gpost_ref[...]

    norm_in(LRU_NB - 1)
    proj_in(LRU_NB - 1)
    norm_in(0)
    proj_in(0)
    conv_header()
    for k in range(LRU_NB):
        if k + 1 < LRU_NB - 1:
            norm_in(k + 1)
            proj_in(k + 1)
        conv(k)
        gates(k)
        local_scan(k)
    segment_carries()
    for k in range(LRU_NB):
        gate_out(k)
        proj_out(k)

    for cp in _o_copies(obuf, o_hbm, out_sem, b, t, slot):
        cp.start()

    @pl.when(n == n_steps - 1)
    def _():
        for cp in _o_copies(obuf, o_hbm, out_sem, b, t, slot):
            cp.wait()
        for cp in _o_copies(obuf, o_hbm, out_sem, b, t, nslot):
            cp.wait()


def _ret_kernel(x_ref, gpre_ref, gpost_ref, win_hbm, wout_hbm, cos_ref, sin_ref, dmask_ref, qdec_ref,
                kdec_ref, cdec_ref, o_ref, w_sem, win_ref, wout_ref, h_ref, u2_ref, y_ref, state_ref):
    ts = x_ref.shape[1]
    t = pl.program_id(1)

    @pl.when((pl.program_id(0) == 0) & (t == 0))
    def _():
        w_copies = [pltpu.make_async_copy(win_hbm, win_ref, w_sem.at[0]),
                    pltpu.make_async_copy(wout_hbm, wout_ref, w_sem.at[1])]
        for cp in w_copies:
            cp.start()
        for cp in w_copies:
            cp.wait()

    @pl.when(t == 0)
    def _():
        state_ref[...] = jnp.zeros_like(state_ref)

    x = x_ref[0]
    h_ref[...] = (x * _rms_scale(x) * gpre_ref[...]).astype(BF16)
    half = QK_HEAD // 2
    u_bufs = (u2_ref.at[0], u2_ref.at[1])
    u_bufs[0][...] = jnp.dot(h_ref[...], win_ref[0], preferred_element_type=F32)
    for hd in range(RET_HEADS):
        if hd + 1 < RET_HEADS:
            u_bufs[(hd + 1) % 2][...] = jnp.dot(h_ref[...], win_ref[hd + 1], preferred_element_type=F32)
        u_ref = u_bufs[hd % 2]
        for c in range(ts // CHUNK):
            rows = slice(c * CHUNK, (c + 1) * CHUNK)
            cos = cos_ref[rows, :]
            sin = sin_ref[rows, :]
            q1 = u_ref[rows, 0:half]
            q2 = u_ref[rows, half:QK_HEAD]
            k1 = u_ref[rows, QK_HEAD:QK_HEAD + half]
            k2 = u_ref[rows, QK_HEAD + half:2 * QK_HEAD]
            qr1 = q1 * cos - q2 * sin
            qr2 = q2 * cos + q1 * sin
            kr1 = (k1 * cos - k2 * sin) * (QK_HEAD ** -0.5)
            kr2 = (k2 * cos + k1 * sin) * (QK_HEAD ** -0.5)
            qdec = qdec_ref[hd]
            kdec = kdec_ref[hd]
            q = jnp.concatenate([qr1, qr2], axis=-1).astype(BF16)
            qd = jnp.concatenate([qr1 * qdec, qr2 * qdec], axis=-1).astype(BF16)
            k = jnp.concatenate([kr1, kr2], axis=-1).astype(BF16)
            kd = jnp.concatenate([kr1 * kdec, kr2 * kdec], axis=-1).astype(BF16)
            v = u_ref[rows, 2 * QK_HEAD:2 * QK_HEAD + V_HEAD].astype(BF16)
            s = lax.dot_general(q, k, (((1,), (1,)), ((), ())), preferred_element_type=F32)
            s = (s * dmask_ref[hd]).astype(BF16)
            inner = jnp.dot(s, v, preferred_element_type=F32)
            st = state_ref[hd]
            cross = jnp.dot(qd, st.astype(BF16), preferred_element_type=F32)
            state_ref[hd] = st * cdec_ref[hd] + lax.dot_general(
                kd, v, (((0,), (0,)), ((), ())), preferred_element_type=F32)
            o = inner + cross
            mu = jnp.mean(o, axis=-1, keepdims=True)
            oc = o - mu
            var = jnp.mean(oc * oc, axis=-1, keepdims=True)
            gn = oc * lax.rsqrt(var + GN_EPS)
            gate = u_ref[rows, 2 * QK_HEAD + V_HEAD:HEAD_COLS]
            y_ref[rows, hd * V_HEAD:(hd + 1) * V_HEAD] = (gn * (gate * _sigmoid(gate))).astype(BF16)

    o = jnp.dot(y_ref[...], wout_ref[...], preferred_element_type=F32)
    o_ref[0] = x_ref[0] + o * _rms_scale(o) * gpost_ref[...]


def _const_spec(shape):
    nd = len(shape)
    return pl.BlockSpec(shape, lambda b, t: (0,) * nd, pipeline_mode=pl.Buffered(1))


def _rows8(v):
    return jnp.broadcast_to(v[None, :], (SUBLANES, v.shape[0]))


def _lru_layer(x, gpre, gpost, w_in, conv_w, conv_b, wg, b_a, b_x, lam, w_out):
    bsz, s, _ = x.shape
    any_spec = pl.BlockSpec(memory_space=pl.ANY)
    return pl.pallas_call(
        _lru_kernel,
        grid=(bsz, s // LRU_TS),
        in_specs=[any_spec, _const_spec((SUBLANES, D_MODEL)), _const_spec((SUBLANES, D_MODEL)),
                  any_spec, _const_spec((CONV_W * SUBLANES, D_RNN)), _const_spec((SUBLANES, D_RNN)),
                  any_spec, _const_spec((SUBLANES, D_RNN)), _const_spec((SUBLANES, D_RNN)),
                  _const_spec((SUBLANES, D_RNN)), any_spec],
        out_specs=any_spec,
        out_shape=jax.ShapeDtypeStruct(x.shape, x.dtype),
        scratch_shapes=[
            pltpu.VMEM((2, LRU_P, SUBLANES, D_MODEL), F32),
            pltpu.VMEM((2, LRU_P, SUBLANES, D_MODEL), F32),
            pltpu.SemaphoreType.DMA((2,)),
            pltpu.SemaphoreType.DMA((2,)),
            pltpu.SemaphoreType.DMA((3,)),
            pltpu.VMEM((D_MODEL, 2 * D_RNN), BF16),
            pltpu.VMEM((N_GATE_GROUPS, GATE_GROUP_W, 2 * GATE_GROUP_W), BF16),
            pltpu.VMEM((D_RNN, D_MODEL), BF16),
            pltpu.VMEM((LRU_TS, D_MODEL), BF16),
            pltpu.VMEM((LRU_TS + (CONV_W - 1) * SUBLANES, D_RNN), F32),
            pltpu.VMEM((LRU_TS, D_RNN), F32),
            pltpu.VMEM((LRU_TS, D_RNN), F32),
            pltpu.VMEM((LRU_TS, D_RNN), BF16),
            pltpu.VMEM((LRU_TS, 2 * D_RNN), F32),
            pltpu.VMEM((LRU_TS, D_RNN), BF16),
            pltpu.VMEM(((CONV_W - 1) * SUBLANES, D_RNN), F32),
            pltpu.VMEM((SUBLANES, D_RNN), F32),
            pltpu.VMEM((SUBLANES, D_RNN), F32),
            pltpu.VMEM((SUBLANES, D_RNN), F32),
            pltpu.VMEM((2, SUBLANES, D_RNN), F32),
        ],
        compiler_params=pltpu.CompilerParams(
            dimension_semantics=("arbitrary", "arbitrary"), vmem_limit_bytes=VMEM_LIMIT),
        name="lru_layer",
    )(x, _rows8(gpre), _rows8(gpost), w_in, jnp.repeat(0.5 * conv_w, SUBLANES, axis=0), _rows8(0.5 * conv_b),
      wg, _rows8(0.5 * b_a), _rows8(0.5 * b_x), _rows8(lam), w_out)


def _ret_layer(x, gpre, gpost, w_in_heads, w_out, cos, sin, dmask, qdec, kdec, cdec):
    bsz, s, _ = x.shape
    ts = RET_TS
    x_spec = pl.BlockSpec((1, ts, D_MODEL), lambda b, t: (b, t, 0))
    rope_spec = pl.BlockSpec((ts, QK_HEAD // 2), lambda b, t: (t, 0))
    return pl.pallas_call(
        _ret_kernel,
        grid=(bsz, s // ts),
        in_specs=[x_spec, _const_spec((1, D_MODEL)), _const_spec((1, D_MODEL)),
                  pl.BlockSpec(memory_space=pl.ANY), pl.BlockSpec(memory_space=pl.ANY),
                  rope_spec, rope_spec,
                  _const_spec((RET_HEADS, CHUNK, CHUNK)), _const_spec((RET_HEADS, CHUNK, QK_HEAD // 2)),
                  _const_spec((RET_HEADS, CHUNK, QK_HEAD // 2)),
                  pl.BlockSpec(memory_space=pltpu.SMEM)],
        out_specs=x_spec,
        out_shape=jax.ShapeDtypeStruct(x.shape, x.dtype),
        scratch_shapes=[
            pltpu.SemaphoreType.DMA((2,)),
            pltpu.VMEM((RET_HEADS, D_MODEL, HEAD_COLS), BF16),
            pltpu.VMEM((V_DIM, D_MODEL), BF16),
            pltpu.VMEM((ts, D_MODEL), BF16),
            pltpu.VMEM((2, ts, HEAD_COLS), F32),
            pltpu.VMEM((ts, V_DIM), BF16),
            pltpu.VMEM((RET_HEADS, QK_HEAD, V_HEAD), F32),
        ],
        compiler_params=pltpu.CompilerParams(
            dimension_semantics=("arbitrary", "arbitrary"), vmem_limit_bytes=VMEM_LIMIT),
        name="ret_layer",
    )(x, gpre, gpost, w_in_heads, w_out, cos, sin, dmask, qdec, kdec, cdec)


def _gate_group_weights(w_a, w_x):
    def bd(w):
        w = w.reshape(N_GATE_GROUPS, GATE_GROUP_BLOCKS, BLOCK_W, BLOCK_W)
        eye = jnp.eye(GATE_GROUP_BLOCKS, dtype=w.dtype)
        full = w[:, :, :, None, :] * eye[None, :, None, :, None]
        return full.reshape(N_GATE_GROUPS, GATE_GROUP_W, GATE_GROUP_W)
    return jnp.concatenate([bd(w_a), bd(w_x)], axis=-1).astype(BF16)


def _ret_head_weights(w_in):
    q = w_in[:, :QK_DIM].reshape(D_MODEL, RET_HEADS, QK_HEAD)
    k = w_in[:, QK_DIM:2 * QK_DIM].reshape(D_MODEL, RET_HEADS, QK_HEAD)
    v = w_in[:, 2 * QK_DIM:2 * QK_DIM + V_DIM].reshape(D_MODEL, RET_HEADS, V_HEAD)
    g = w_in[:, 2 * QK_DIM + V_DIM:].reshape(D_MODEL, RET_HEADS, V_HEAD)
    return jnp.concatenate([q, k, v, g], axis=-1).transpose(1, 0, 2).astype(BF16)


def _retention_tables(s):
    pos = jnp.arange(s, dtype=F32)
    inv_freq = ROPE_BASE ** (-jnp.arange(0, QK_HEAD, 2, dtype=F32) / QK_HEAD)
    ang = pos[:, None] * inv_freq[None, :]
    cos, sin = jnp.cos(ang), jnp.sin(ang)
    log_g = jnp.log(1.0 - jnp.exp2(-5.0 - jnp.arange(RET_HEADS, dtype=F32)))
    idx = jnp.arange(CHUNK, dtype=F32)
    diff = idx[:, None] - idx[None, :]
    dmask = jnp.where(diff[None] >= 0, jnp.exp(jnp.maximum(diff, 0.0)[None] * log_g[:, None, None]), 0.0)
    q_decay = jnp.exp((idx[None, :] + 1.0) * log_g[:, None])[..., None]
    k_decay = jnp.exp((CHUNK - 1.0 - idx[None, :]) * log_g[:, None])[..., None]
    qdec = jnp.broadcast_to(q_decay, (RET_HEADS, CHUNK, QK_HEAD // 2))
    kdec = jnp.broadcast_to(k_decay, (RET_HEADS, CHUNK, QK_HEAD // 2))
    cdec = jnp.exp(CHUNK * log_g)
    return cos, sin, dmask, qdec, kdec, cdec


def kernel(x, norm_pre, norm_post, lru_w_in, lru_conv_w, lru_conv_b, lru_w_a, lru_b_a, lru_w_x, lru_b_x,
           lru_lambda, lru_w_out, ret_w_in, ret_w_out):
    s = x.shape[1]
    tables = _retention_tables(s)
    for layer in range(DEPTH):
        j = layer // 2
        if layer % 2 == 0:
            w_in = jnp.concatenate([lru_w_in[j][:, :D_RNN], 0.5 * lru_w_in[j][:, D_RNN:]], axis=1).astype(BF16)
            x = _lru_layer(x, norm_pre[layer], norm_post[layer], w_in, lru_conv_w[j],
                           lru_conv_b[j], _gate_group_weights(lru_w_a[j], lru_w_x[j]), lru_b_a[j],
                           lru_b_x[j], lru_lambda[j], lru_w_out[j].astype(BF16))
        else:
            x = _ret_layer(x, norm_pre[layer][None, :], norm_post[layer][None, :],
                           _ret_head_weights(ret_w_in[j]), ret_w_out[j].astype(BF16), *tables)
    return x
```

```python
import jax
import jax.numpy as jnp
from jax import lax
from jax.experimental import pallas as pl
from jax.experimental.pallas import tpu as pltpu

D_MODEL = 1024
DEPTH = 4
D_RNN = 1536
N_GATE_BLOCKS = 16
BLOCK_W = 96
CONV_W = 4
C_RG = 8.0
RET_HEADS = 4
QK_HEAD = 256
V_HEAD = 512
QK_DIM = RET_HEADS * QK_HEAD
V_DIM = RET_HEADS * V_HEAD
CHUNK = 256
ROPE_BASE = 10000.0
RMS_EPS = 1e-6
GN_EPS = 1e-5

GATE_GROUP_BLOCKS = 4
GATE_GROUP_W = GATE_GROUP_BLOCKS * BLOCK_W
N_GATE_GROUPS = N_GATE_BLOCKS // GATE_GROUP_BLOCKS
HEAD_COLS = 2 * QK_HEAD + 2 * V_HEAD
SUBLANES = 8

LRU_TS = 512
LRU_P = LRU_TS // SUBLANES
LRU_BLK = 256
LRU_RG = LRU_BLK // SUBLANES
LRU_NB = LRU_TS // LRU_BLK
LRU_SCAN_RG = 8
LRU_PROJ_PIECES = 6
LRU_PROJ_W = 2 * D_RNN // LRU_PROJ_PIECES
RET_TS = 512
VMEM_LIMIT = 56 * 1024 * 1024

F32 = jnp.float32
BF16 = jnp.bfloat16
LOG2_E = 1.4426950408889634
SQRT_TINY = 1e-30


def _sigmoid(v):
    return 0.5 * jnp.tanh(0.5 * v) + 0.5


def _rms_scale(v):
    return lax.rsqrt(jnp.mean(v * v, axis=-1, keepdims=True) + RMS_EPS)


def _shift_rows(v, d, fill, rowid):
    return jnp.where(rowid >= d, pltpu.roll(v, d, axis=0), fill)


def _scan8(a, b, rowid):
    for d in (1, 2, 4):
        b = a * _shift_rows(b, d, 0.0, rowid) + b
        a = a * _shift_rows(a, d, 1.0, rowid)
    return a, b


def _x_copies(x_hbm, xbuf, sem, b, t, slot):
    return [pltpu.make_async_copy(x_hbm.at[b, pl.ds(t * LRU_TS + s * LRU_P, LRU_P), :],
                                  xbuf.at[slot, :, s, :], sem.at[slot]) for s in range(SUBLANES)]


def _o_copies(obuf, o_hbm, sem, b, t, slot):
    return [pltpu.make_async_copy(obuf.at[slot, :, s, :],
                                  o_hbm.at[b, pl.ds(t * LRU_TS + s * LRU_P, LRU_P), :], sem.at[slot])
            for s in range(SUBLANES)]


def _lru_kernel(x_hbm, gpre_ref, gpost_ref, win_hbm, convw_ref, convb_ref, wg_hbm, hba_ref, hbx_ref,
                lam_ref, wout_hbm, o_hbm, xbuf, obuf, in_sem, out_sem, w_sem, win_ref, wg_ref, wout_ref,
                h16_ref, xp_ref, gate_ref, sg_ref, xb_ref, xb16_ref, g_ref, y_ref, tail_ref, hc_ref, cin_ref,
                c2_ref, carry_ref):
    nb_t = x_hbm.shape[1] // LRU_TS
    n_steps = x_hbm.shape[0] * nb_t
    b = pl.program_id(0)
    t = pl.program_id(1)
    n = b * nb_t + t
    cur = n % 2
    nxt = 1 - cur
    xs = n % 3
    xn = (n + 1) % 3

    def x_copies(m, slot):
        return _x_copies(x_hbm, xbuf, in_sem, m // nb_t, m % nb_t, slot)

    hdr = (CONV_W - 1) * SUBLANES

    def norm_in(xslot, k):
        for ip in range(LRU_RG // 2):
            i0 = k * LRU_RG + 2 * ip
            hs = []
            for d in range(2):
                v = xbuf[xslot, i0 + d]
                hs.append(v * _rms_scale(v) * gpre_ref[...])
            h16_ref[SUBLANES * i0:SUBLANES * (i0 + 2), :] = jnp.concatenate(hs, axis=0).astype(BF16)

    def proj_in(c):
        cols = slice(LRU_PROJ_W * c, LRU_PROJ_W * (c + 1))
        res = jnp.dot(h16_ref[...], win_ref[:, cols], preferred_element_type=F32)
        if c < LRU_PROJ_PIECES // 2:
            xp_ref[hdr:, cols] = res
        else:
            gate_ref[:, LRU_PROJ_W * c - D_RNN:LRU_PROJ_W * (c + 1) - D_RNN] = res

    @pl.when(n == 0)
    def _():
        for cp in x_copies(n, xs):
            cp.start()
        for cp in x_copies(n + 1, xn):
            cp.start()
        w_copies = [pltpu.make_async_copy(src, dst, w_sem.at[i]) for i, (src, dst) in enumerate(
            ((win_hbm, win_ref), (wg_hbm, wg_ref), (wout_hbm, wout_ref)))]
        for cp in w_copies:
            cp.start()
        for cp in w_copies:
            cp.wait()
        for cp in x_copies(n, xs):
            cp.wait()
        for k in range(LRU_NB):
            norm_in(xs, k)
        for c in range(LRU_PROJ_PIECES):
            proj_in(c)

    @pl.when(n + 2 < n_steps)
    def _():
        for cp in x_copies(n + 2, (n + 2) % 3):
            cp.start()

    @pl.when(n >= 2)
    def _():
        for cp in _o_copies(obuf, o_hbm, out_sem, b, t, cur):
            cp.wait()

    @pl.when(t == 0)
    def _():
        tail_ref[...] = jnp.zeros_like(tail_ref)
        hc_ref[...] = jnp.zeros_like(hc_ref)

    @pl.when(n + 1 < n_steps)
    def _():
        for cp in x_copies(n + 1, xn):
            cp.wait()

    lam = lam_ref[...]
    c2_ref[...] = (-0.5 * C_RG * LOG2_E) * (jnp.maximum(-lam, 0.0) + jnp.log1p(jnp.exp(-jnp.abs(lam))))
    rowid = lax.broadcasted_iota(jnp.int32, (SUBLANES, GATE_GROUP_W), 0)
    rowid_full = lax.broadcasted_iota(jnp.int32, (SUBLANES, D_RNN), 0)

    def conv_header():
        for g in range(CONV_W - 1):
            rows = slice(SUBLANES * g, SUBLANES * (g + 1))
            src = hdr + SUBLANES * (LRU_P - (CONV_W - 1) + g)
            cur_rows = xp_ref[src:src + SUBLANES, :]
            merged = jnp.where(rowid_full == SUBLANES - 1, tail_ref[rows, :], cur_rows)
            xp_ref[rows, :] = pltpu.roll(merged, 1, axis=0)
            tail_ref[rows, :] = cur_rows

    def conv(k):
        for ip in range(LRU_RG // 2):
            i0 = k * LRU_RG + 2 * ip
            outs = []
            for d in range(2):
                r0 = SUBLANES * (i0 + d)
                acc = convb_ref[...]
                for j in range(CONV_W):
                    acc = acc + convw_ref[SUBLANES * j:SUBLANES * (j + 1), :] * xp_ref[
                        r0 + SUBLANES * j:r0 + SUBLANES * (j + 1), :]
                xb_ref[r0:r0 + SUBLANES, :] = acc
                outs.append(acc)
                hg = gate_ref[r0:r0 + SUBLANES, :]
                sg_ref[r0:r0 + SUBLANES, :] = hg * jnp.tanh(hg) + hg
            xb16_ref[SUBLANES * i0:SUBLANES * (i0 + 2), :] = jnp.concatenate(outs, axis=0).astype(BF16)

    def gates(k):
        rows = slice(LRU_BLK * k, LRU_BLK * (k + 1))
        for g in range(N_GATE_GROUPS):
            g_ref[rows, 2 * GATE_GROUP_W * g:2 * GATE_GROUP_W * (g + 1)] = jnp.dot(
                xb16_ref[rows, GATE_GROUP_W * g:GATE_GROUP_W * (g + 1)], wg_ref[g],
                preferred_element_type=F32)

    def local_scan(j):
        for g in range(N_GATE_GROUPS):
            cs = slice(GATE_GROUP_W * g, GATE_GROUP_W * (g + 1))
            ca = slice(2 * GATE_GROUP_W * g, 2 * GATE_GROUP_W * g + GATE_GROUP_W)
            cx = slice(2 * GATE_GROUP_W * g + GATE_GROUP_W, 2 * GATE_GROUP_W * (g + 1))
            if j == 0:
                hl = jnp.zeros((SUBLANES, GATE_GROUP_W), F32)
                ac = jnp.ones((SUBLANES, GATE_GROUP_W), F32)
            else:
                hl = carry_ref[0, :, cs]
                ac = carry_ref[1, :, cs]
            c2 = c2_ref[:, cs]
            hba = hba_ref[:, cs]
            hbx = hbx_ref[:, cs]
            for i in range(LRU_SCAN_RG):
                rows = slice(SUBLANES * (j * LRU_SCAN_RG + i), SUBLANES * (j * LRU_SCAN_RG + i + 1))
                tr = jnp.tanh(g_ref[rows, ca] + hba)
                ti = jnp.tanh(g_ref[rows, cx] + hbx)
                a = jnp.exp2(c2 * tr + c2)
                z = 1.0 - a * a
                bt = (z * lax.rsqrt(jnp.maximum(z, SQRT_TINY))) * ((ti + 1.0) * xb_ref[rows, cs])
                hl = a * hl + bt
                ac = a * ac
                g_ref[rows, ca] = hl
                g_ref[rows, cx] = ac
            if j + 1 < LRU_P // LRU_SCAN_RG:
                carry_ref[0, :, cs] = hl
                carry_ref[1, :, cs] = ac

    def segment_carries():
        last = slice(LRU_TS - SUBLANES, LRU_TS)
        for g in range(N_GATE_GROUPS):
            cs = slice(GATE_GROUP_W * g, GATE_GROUP_W * (g + 1))
            ca = slice(2 * GATE_GROUP_W * g, 2 * GATE_GROUP_W * g + GATE_GROUP_W)
            cx = slice(2 * GATE_GROUP_W * g + GATE_GROUP_W, 2 * GATE_GROUP_W * (g + 1))
            hprev = hc_ref[:, cs]
            acum, bcum = _scan8(g_ref[last, cx], g_ref[last, ca], rowid)
            hend = acum * hprev + bcum
            cin_ref[:, cs] = jnp.where(rowid == 0, hprev, pltpu.roll(hend, 1, axis=0))
            hc_ref[:, cs] = jnp.broadcast_to(hend[SUBLANES - 1:SUBLANES, :], hend.shape)

    def gate_out(k):
        for g in range(N_GATE_GROUPS):
            cs = slice(GATE_GROUP_W * g, GATE_GROUP_W * (g + 1))
            ca = slice(2 * GATE_GROUP_W * g, 2 * GATE_GROUP_W * g + GATE_GROUP_W)
            cx = slice(2 * GATE_GROUP_W * g + GATE_GROUP_W, 2 * GATE_GROUP_W * (g + 1))
            cin = cin_ref[:, cs]
            for ip in range(LRU_RG // 2):
                ys = []
                for d in range(2):
                    r0 = SUBLANES * (k * LRU_RG + 2 * ip + d)
                    rows = slice(r0, r0 + SUBLANES)
                    hseq = g_ref[rows, ca] + g_ref[rows, cx] * cin
                    ys.append(hseq * sg_ref[rows, cs])
                r0 = SUBLANES * (k * LRU_RG + 2 * ip)
                y_ref[r0:r0 + 2 * SUBLANES, cs] = jnp.concatenate(ys, axis=0).astype(BF16)

    def proj_out(k):
        o = jnp.dot(y_ref[LRU_BLK * k:LRU_BLK * (k + 1), :], wout_ref[...], preferred_element_type=F32)
        for i in range(LRU_RG):
            ov = o[SUBLANES * i:SUBLANES * (i + 1)]
            obuf[cur, k * LRU_RG + i] = xbuf[xs, k * LRU_RG + i] + ov * _rms_scale(ov) * gpost_ref[...]

    conv_header()
    conv(0)
    gates(0)
    conv(1)
    norm_in(xn, 0)
    norm_in(xn, 1)
    mxu_pieces = [lambda c=c: proj_in(c) for c in range(LRU_PROJ_PIECES)]
    mxu_pieces.insert(2, lambda: gates(1))
    n_slices = LRU_P // LRU_SCAN_RG
    for j in range(max(n_slices, len(mxu_pieces))):
        if j < len(mxu_pieces):
            mxu_pieces[j]()
        if j < n_slices:
            local_scan(j)
    segment_carries()
    for k in range(LRU_NB):
        gate_out(k)
        proj_out(k)

    for cp in _o_copies(obuf, o_hbm, out_sem, b, t, cur):
        cp.start()

    @pl.when(n == n_steps - 1)
    def _():
        for cp in _o_copies(obuf, o_hbm, out_sem, b, t, cur):
            cp.wait()
        for cp in _o_copies(obuf, o_hbm, out_sem, b, t, nxt):
            cp.wait()


def _ret_kernel(x_ref, gpre_ref, gpost_ref, win_hbm, wout_hbm, cos_ref, sin_ref, dmask_ref, qdec_ref,
                kdec_ref, cdec_ref, o_ref, w_sem, win_ref, wout_ref, h_ref, u2_ref, y_ref, state_ref):
    ts = x_ref.shape[1]
    t = pl.program_id(1)

    @pl.when((pl.program_id(0) == 0) & (t == 0))
    def _():
        w_copies = [pltpu.make_async_copy(win_hbm, win_ref, w_sem.at[0]),
                    pltpu.make_async_copy(wout_hbm, wout_ref, w_sem.at[1])]
        for cp in w_copies:
            cp.start()
        for cp in w_copies:
            cp.wait()

    @pl.when(t == 0)
    def _():
        state_ref[...] = jnp.zeros_like(state_ref)

    x = x_ref[0]
    h_ref[...] = (x * _rms_scale(x) * gpre_ref[...]).astype(BF16)
    half = QK_HEAD // 2
    u_bufs = (u2_ref.at[0], u2_ref.at[1])
    u_bufs[0][...] = jnp.dot(h_ref[...], win_ref[0], preferred_element_type=F32)
    for hd in range(RET_HEADS):
        if hd + 1 < RET_HEADS:
            u_bufs[(hd + 1) % 2][...] = jnp.dot(h_ref[...], win_ref[hd + 1], preferred_element_type=F32)
        u_ref = u_bufs[hd % 2]
        for c in range(ts // CHUNK):
            rows = slice(c * CHUNK, (c + 1) * CHUNK)
            cos = cos_ref[rows, :]
            sin = sin_ref[rows, :]
            q1 = u_ref[rows, 0:half]
            q2 = u_ref[rows, half:QK_HEAD]
            k1 = u_ref[rows, QK_HEAD:QK_HEAD + half]
            k2 = u_ref[rows, QK_HEAD + half:2 * QK_HEAD]
            qr1 = q1 * cos - q2 * sin
            qr2 = q2 * cos + q1 * sin
            kr1 = (k1 * cos - k2 * sin) * (QK_HEAD ** -0.5)
            kr2 = (k2 * cos + k1 * sin) * (QK_HEAD ** -0.5)
            qdec = qdec_ref[hd]
            kdec = kdec_ref[hd]
            q = jnp.concatenate([qr1, qr2], axis=-1).astype(BF16)
            qd = jnp.concatenate([qr1 * qdec, qr2 * qdec], axis=-1).astype(BF16)
            k = jnp.concatenate([kr1, kr2], axis=-1).astype(BF16)
            kd = jnp.concatenate([kr1 * kdec, kr2 * kdec], axis=-1).astype(BF16)
            v = u_ref[rows, 2 * QK_HEAD:2 * QK_HEAD + V_HEAD].astype(BF16)
            s = lax.dot_general(q, k, (((1,), (1,)), ((), ())), preferred_element_type=F32)
            s = (s * dmask_ref[hd]).astype(BF16)
            inner = jnp.dot(s, v, preferred_element_type=F32)
            st = state_ref[hd]
            cross = jnp.dot(qd, st.astype(BF16), preferred_element_type=F32)
            state_ref[hd] = st * cdec_ref[hd] + lax.dot_general(
                kd, v, (((0,), (0,)), ((), ())), preferred_element_type=F32)
            o = inner + cross
            mu = jnp.mean(o, axis=-1, keepdims=True)
            oc = o - mu
            var = jnp.mean(oc * oc, axis=-1, keepdims=True)
            gn = oc * lax.rsqrt(var + GN_EPS)
            gate = u_ref[rows, 2 * QK_HEAD + V_HEAD:HEAD_COLS]
            y_ref[rows, hd * V_HEAD:(hd + 1) * V_HEAD] = (gn * (gate * _sigmoid(gate))).astype(BF16)

    o = jnp.dot(y_ref[...], wout_ref[...], preferred_element_type=F32)
    o_ref[0] = x_ref[0] + o * _rms_scale(o) * gpost_ref[...]


def _const_spec(shape):
    nd = len(shape)
    return pl.BlockSpec(shape, lambda b, t: (0,) * nd, pipeline_mode=pl.Buffered(1))


def _rows8(v):
    return jnp.broadcast_to(v[None, :], (SUBLANES, v.shape[0]))


def _lru_layer(x, gpre, gpost, w_in, conv_w, conv_b, wg, b_a, b_x, lam, w_out):
    bsz, s, _ = x.shape
    any_spec = pl.BlockSpec(memory_space=pl.ANY)
    return pl.pallas_call(
        _lru_kernel,
        grid=(bsz, s // LRU_TS),
        in_specs=[any_spec, _const_spec((SUBLANES, D_MODEL)), _const_spec((SUBLANES, D_MODEL)),
                  any_spec, _const_spec((CONV_W * SUBLANES, D_RNN)), _const_spec((SUBLANES, D_RNN)),
                  any_spec, _const_spec((SUBLANES, D_RNN)), _const_spec((SUBLANES, D_RNN)),
                  _const_spec((SUBLANES, D_RNN)), any_spec],
        out_specs=any_spec,
        out_shape=jax.ShapeDtypeStruct(x.shape, x.dtype),
        scratch_shapes=[
            pltpu.VMEM((3, LRU_P, SUBLANES, D_MODEL), F32),
            pltpu.VMEM((2, LRU_P, SUBLANES, D_MODEL), F32),
            pltpu.SemaphoreType.DMA((3,)),
            pltpu.SemaphoreType.DMA((2,)),
            pltpu.SemaphoreType.DMA((3,)),
            pltpu.VMEM((D_MODEL, 2 * D_RNN), BF16),
            pltpu.VMEM((N_GATE_GROUPS, GATE_GROUP_W, 2 * GATE_GROUP_W), BF16),
            pltpu.VMEM((D_RNN, D_MODEL), BF16),
            pltpu.VMEM((LRU_TS, D_MODEL), BF16),
            pltpu.VMEM((LRU_TS + (CONV_W - 1) * SUBLANES, D_RNN), F32),
            pltpu.VMEM((LRU_TS, D_RNN), F32),
            pltpu.VMEM((LRU_TS, D_RNN), F32),
            pltpu.VMEM((LRU_TS, D_RNN), F32),
            pltpu.VMEM((LRU_TS, D_RNN), BF16),
            pltpu.VMEM((LRU_TS, 2 * D_RNN), F32),
            pltpu.VMEM((LRU_TS, D_RNN), BF16),
            pltpu.VMEM(((CONV_W - 1) * SUBLANES, D_RNN), F32),
            pltpu.VMEM((SUBLANES, D_RNN), F32),
            pltpu.VMEM((SUBLANES, D_RNN), F32),
            pltpu.VMEM((SUBLANES, D_RNN), F32),
            pltpu.VMEM((2, SUBLANES, D_RNN), F32),
        ],
        compiler_params=pltpu.CompilerParams(
            dimension_semantics=("arbitrary", "arbitrary"), vmem_limit_bytes=VMEM_LIMIT),
        name="lru_layer",
    )(x, _rows8(gpre), _rows8(gpost), w_in, jnp.repeat(0.5 * conv_w, SUBLANES, axis=0), _rows8(0.5 * conv_b),
      wg, _rows8(0.5 * b_a), _rows8(0.5 * b_x), _rows8(lam), w_out)


def _ret_layer(x, gpre, gpost, w_in_heads, w_out, cos, sin, dmask, qdec, kdec, cdec):
    bsz, s, _ = x.shape
    ts = RET_TS
    x_spec = pl.BlockSpec((1, ts, D_MODEL), lambda b, t: (b, t, 0))
    rope_spec = pl.BlockSpec((ts, QK_HEAD // 2), lambda b, t: (t, 0))
    return pl.pallas_call(
        _ret_kernel,
        grid=(bsz, s // ts),
        in_specs=[x_spec, _const_spec((1, D_MODEL)), _const_spec((1, D_MODEL)),
                  pl.BlockSpec(memory_space=pl.ANY), pl.BlockSpec(memory_space=pl.ANY),
                  rope_spec, rope_spec,
                  _const_spec((RET_HEADS, CHUNK, CHUNK)), _const_spec((RET_HEADS, CHUNK, QK_HEAD // 2)),
                  _const_spec((RET_HEADS, CHUNK, QK_HEAD // 2)),
                  pl.BlockSpec(memory_space=pltpu.SMEM)],
        out_specs=x_spec,
        out_shape=jax.ShapeDtypeStruct(x.shape, x.dtype),
        scratch_shapes=[
            pltpu.SemaphoreType.DMA((2,)),
            pltpu.VMEM((RET_HEADS, D_MODEL, HEAD_COLS), BF16),
            pltpu.VMEM((V_DIM, D_MODEL), BF16),
            pltpu.VMEM((ts, D_MODEL), BF16),
            pltpu.VMEM((2, ts, HEAD_COLS), F32),
            pltpu.VMEM((ts, V_DIM), BF16),
            pltpu.VMEM((RET_HEADS, QK_HEAD, V_HEAD), F32),
        ],
        compiler_params=pltpu.CompilerParams(
            dimension_semantics=("arbitrary", "arbitrary"), vmem_limit_bytes=VMEM_LIMIT),
        name="ret_layer",
    )(x, gpre, gpost, w_in_heads, w_out, cos, sin, dmask, qdec, kdec, cdec)


def _gate_group_weights(w_a, w_x):
    def bd(w):
        w = w.reshape(N_GATE_GROUPS, GATE_GROUP_BLOCKS, BLOCK_W, BLOCK_W)
        eye = jnp.eye(GATE_GROUP_BLOCKS, dtype=w.dtype)
        full = w[:, :, :, None, :] * eye[None, :, None, :, None]
        return full.reshape(N_GATE_GROUPS, GATE_GROUP_W, GATE_GROUP_W)
    return jnp.concatenate([bd(w_a), bd(w_x)], axis=-1).astype(BF16)


def _ret_head_weights(w_in):
    q = w_in[:, :QK_DIM].reshape(D_MODEL, RET_HEADS, QK_HEAD)
    k = w_in[:, QK_DIM:2 * QK_DIM].reshape(D_MODEL, RET_HEADS, QK_HEAD)
    v = w_in[:, 2 * QK_DIM:2 * QK_DIM + V_DIM].reshape(D_MODEL, RET_HEADS, V_HEAD)
    g = w_in[:, 2 * QK_DIM + V_DIM:].reshape(D_MODEL, RET_HEADS, V_HEAD)
    return jnp.concatenate([q, k, v, g], axis=-1).transpose(1, 0, 2).astype(BF16)


def _retention_tables(s):
    pos = jnp.arange(s, dtype=F32)
    inv_freq = ROPE_BASE ** (-jnp.arange(0, QK_HEAD, 2, dtype=F32) / QK_HEAD)
    ang = pos[:, None] * inv_freq[None, :]
    cos, sin = jnp.cos(ang), jnp.sin(ang)
    log_g = jnp.log(1.0 - jnp.exp2(-5.0 - jnp.arange(RET_HEADS, dtype=F32)))
    idx = jnp.arange(CHUNK, dtype=F32)
    diff = idx[:, None] - idx[None, :]
    dmask = jnp.where(diff[None] >= 0, jnp.exp(jnp.maximum(diff, 0.0)[None] * log_g[:, None, None]), 0.0)
    q_decay = jnp.exp((idx[None, :] + 1.0) * log_g[:, None])[..., None]
    k_decay = jnp.exp((CHUNK - 1.0 - idx[None, :]) * log_g[:, None])[..., None]
    qdec = jnp.broadcast_to(q_decay, (RET_HEADS, CHUNK, QK_HEAD // 2))
    kdec = jnp.broadcast_to(k_decay, (RET_HEADS, CHUNK, QK_HEAD // 2))
    cdec = jnp.exp(CHUNK * log_g)
    return cos, sin, dmask, qdec, kdec, cdec


def kernel(x, norm_pre, norm_post, lru_w_in, lru_conv_w, lru_conv_b, lru_w_a, lru_b_a, lru_w_x, lru_b_x,
           lru_lambda, lru_w_out, ret_w_in, ret_w_out):
    s = x.shape[1]
    tables = _retention_tables(s)
    for layer in range(DEPTH):
        j = layer // 2
        if layer % 2 == 0:
            w_in = jnp.concatenate([lru_w_in[j][:, :D_RNN], 0.5 * lru_w_in[j][:, D_RNN:]], axis=1).astype(BF16)
            x = _lru_layer(x, norm_pre[layer], norm_post[layer], w_in, lru_conv_w[j],
                           lru_conv_b[j], _gate_group_weights(lru_w_a[j], lru_w_x[j]), lru_b_a[j],
                           lru_b_x[j], lru_lambda[j], lru_w_out[j].astype(BF16))
        else:
            x = _ret_layer(x, norm_pre[layer][None, :], norm_post[layer][None, :],
                           _ret_head_weights(ret_w_in[j]), ret_w_out[j].astype(BF16), *tables)
    return x
```

```python
import jax
import jax.numpy as jnp
from jax import lax
from jax.experimental import pallas as pl
from jax.experimental.pallas import tpu as pltpu

D_MODEL = 1024
DEPTH = 4
D_RNN = 1536
N_GATE_BLOCKS = 16
BLOCK_W = 96
CONV_W = 4
C_RG = 8.0
RET_HEADS = 4
QK_HEAD = 256
V_HEAD = 512
QK_DIM = RET_HEADS * QK_HEAD
V_DIM = RET_HEADS * V_HEAD
CHUNK = 256
ROPE_BASE = 10000.0
RMS_EPS = 1e-6
GN_EPS = 1e-5

GATE_GROUP_BLOCKS = 4
GATE_GROUP_W = GATE_GROUP_BLOCKS * BLOCK_W
N_GATE_GROUPS = N_GATE_BLOCKS // GATE_GROUP_BLOCKS
HEAD_COLS = 2 * QK_HEAD + 2 * V_HEAD
SUBLANES = 8

LRU_TS = 512
LRU_P = LRU_TS // SUBLANES
LRU_BLK = 256
LRU_RG = LRU_BLK // SUBLANES
LRU_NB = LRU_TS // LRU_BLK
LRU_SCAN_RG = 8
LRU_PROJ_PIECES = 6
LRU_PROJ_W = 2 * D_RNN // LRU_PROJ_PIECES
RET_TS = 512
VMEM_LIMIT = 56 * 1024 * 1024

F32 = jnp.float32
BF16 = jnp.bfloat16
LOG2_E = 1.4426950408889634
SQRT_TINY = 1e-30


def _sigmoid(v):
    return 0.5 * jnp.tanh(0.5 * v) + 0.5


def _rms_scale(v):
    return lax.rsqrt(jnp.mean(v * v, axis=-1, keepdims=True) + RMS_EPS)


def _shift_rows(v, d, fill, rowid):
    return jnp.where(rowid >= d, pltpu.roll(v, d, axis=0), fill)


def _scan8(a, b, rowid):
    for d in (1, 2, 4):
        b = a * _shift_rows(b, d, 0.0, rowid) + b
        a = a * _shift_rows(a, d, 1.0, rowid)
    return a, b


def _x_copies(x_hbm, xbuf, sem, b, t, slot):
    return [pltpu.make_async_copy(x_hbm.at[b, pl.ds(t * LRU_TS + s * LRU_P, LRU_P), :],
                                  xbuf.at[slot, :, s, :], sem.at[slot]) for s in range(SUBLANES)]


def _o_copies(obuf, o_hbm, sem, b, t, slot):
    return [pltpu.make_async_copy(obuf.at[slot, :, s, :],
                                  o_hbm.at[b, pl.ds(t * LRU_TS + s * LRU_P, LRU_P), :], sem.at[slot])
            for s in range(SUBLANES)]


def _lru_kernel(x_hbm, gpre_ref, gpost_ref, win_hbm, convw_ref, convb_ref, wg_hbm, hba_ref, hbx_ref,
                lam_ref, wout_hbm, o_hbm, xbuf, obuf, in_sem, out_sem, w_sem, win_ref, wg_ref, wout_ref,
                h16_ref, xp_ref, gate_ref, sg_ref, xb_ref, xb16_ref, g_ref, y_ref, tail_ref, hc_ref, cin_ref,
                c2_ref, carry_ref):
    nb_t = x_hbm.shape[1] // LRU_TS
    n_steps = x_hbm.shape[0] * nb_t
    b = pl.program_id(0)
    t = pl.program_id(1)
    n = b * nb_t + t
    cur = n % 2
    nxt = 1 - cur
    xs = n % 3
    xn = (n + 1) % 3

    def x_copies(m, slot):
        return _x_copies(x_hbm, xbuf, in_sem, m // nb_t, m % nb_t, slot)

    hdr = (CONV_W - 1) * SUBLANES

    def norm_in(xslot, k):
        for ip in range(LRU_RG // 2):
            i0 = k * LRU_RG + 2 * ip
            hs = []
            for d in range(2):
                v = xbuf[xslot, i0 + d]
                hs.append(v * _rms_scale(v) * gpre_ref[...])
            h16_ref[SUBLANES * i0:SUBLANES * (i0 + 2), :] = jnp.concatenate(hs, axis=0).astype(BF16)

    def proj_in(c):
        cols = slice(LRU_PROJ_W * c, LRU_PROJ_W * (c + 1))
        res = jnp.dot(h16_ref[...], win_ref[:, cols], preferred_element_type=F32)
        if c < LRU_PROJ_PIECES // 2:
            xp_ref[hdr:, cols] = res
        else:
            gate_ref[:, LRU_PROJ_W * c - D_RNN:LRU_PROJ_W * (c + 1) - D_RNN] = res

    @pl.when(n == 0)
    def _():
        for cp in x_copies(n, xs):
            cp.start()
        for cp in x_copies(n + 1, xn):
            cp.start()
        w_copies = [pltpu.make_async_copy(src, dst, w_sem.at[i]) for i, (src, dst) in enumerate(
            ((win_hbm, win_ref), (wg_hbm, wg_ref), (wout_hbm, wout_ref)))]
        for cp in w_copies:
            cp.start()
        for cp in w_copies:
            cp.wait()
        for cp in x_copies(n, xs):
            cp.wait()
        for k in range(LRU_NB):
            norm_in(xs, k)
        for c in range(LRU_PROJ_PIECES):
            proj_in(c)

    @pl.when(n + 2 < n_steps)
    def _():
        for cp in x_copies(n + 2, (n + 2) % 3):
            cp.start()

    @pl.when(n >= 2)
    def _():
        for cp in _o_copies(obuf, o_hbm, out_sem, b, t, cur):
            cp.wait()

    @pl.when(t == 0)
    def _():
        tail_ref[...] = jnp.zeros_like(tail_ref)
        hc_ref[...] = jnp.zeros_like(hc_ref)

    @pl.when(n + 1 < n_steps)
    def _():
        for cp in x_copies(n + 1, xn):
            cp.wait()

    lam = lam_ref[...]
    c2_ref[...] = (-0.5 * C_RG * LOG2_E) * (jnp.maximum(-lam, 0.0) + jnp.log1p(jnp.exp(-jnp.abs(lam))))
    rowid = lax.broadcasted_iota(jnp.int32, (SUBLANES, GATE_GROUP_W), 0)
    rowid_full = lax.broadcasted_iota(jnp.int32, (SUBLANES, D_RNN), 0)

    def conv_header():
        for g in range(CONV_W - 1):
            rows = slice(SUBLANES * g, SUBLANES * (g + 1))
            src = hdr + SUBLANES * (LRU_P - (CONV_W - 1) + g)
            cur_rows = xp_ref[src:src + SUBLANES, :]
            merged = jnp.where(rowid_full == SUBLANES - 1, tail_ref[rows, :], cur_rows)
            xp_ref[rows, :] = pltpu.roll(merged, 1, axis=0)
            tail_ref[rows, :] = cur_rows

    def conv(k):
        for ip in range(LRU_RG // 2):
            i0 = k * LRU_RG + 2 * ip
            outs = []
            for d in range(2):
                r0 = SUBLANES * (i0 + d)
                acc = convb_ref[...]
                for j in range(CONV_W):
                    acc = acc + convw_ref[SUBLANES * j:SUBLANES * (j + 1), :] * xp_ref[
                        r0 + SUBLANES * j:r0 + SUBLANES * (j + 1), :]
                xb_ref[r0:r0 + SUBLANES, :] = acc
                outs.append(acc)
                hg = gate_ref[r0:r0 + SUBLANES, :]
                sg_ref[r0:r0 + SUBLANES, :] = hg * jnp.tanh(hg) + hg
            xb16_ref[SUBLANES * i0:SUBLANES * (i0 + 2), :] = jnp.concatenate(outs, axis=0).astype(BF16)

    def gates(k):
        rows = slice(LRU_BLK * k, LRU_BLK * (k + 1))
        for g in range(N_GATE_GROUPS):
            g_ref[rows, 2 * GATE_GROUP_W * g:2 * GATE_GROUP_W * (g + 1)] = jnp.dot(
                xb16_ref[rows, GATE_GROUP_W * g:GATE_GROUP_W * (g + 1)], wg_ref[g],
                preferred_element_type=F32)

    def local_scan(j):
        for g in range(N_GATE_GROUPS):
            cs = slice(GATE_GROUP_W * g, GATE_GROUP_W * (g + 1))
            ca = slice(2 * GATE_GROUP_W * g, 2 * GATE_GROUP_W * g + GATE_GROUP_W)
            cx = slice(2 * GATE_GROUP_W * g + GATE_GROUP_W, 2 * GATE_GROUP_W * (g + 1))
            if j == 0:
                hl = jnp.zeros((SUBLANES, GATE_GROUP_W), F32)
                ac = jnp.ones((SUBLANES, GATE_GROUP_W), F32)
            else:
                hl = carry_ref[0, :, cs]
                ac = carry_ref[1, :, cs]
            c2 = c2_ref[:, cs]
            hba = hba_ref[:, cs]
            hbx = hbx_ref[:, cs]
            for i in range(LRU_SCAN_RG):
                rows = slice(SUBLANES * (j * LRU_SCAN_RG + i), SUBLANES * (j * LRU_SCAN_RG + i + 1))
                tr = jnp.tanh(g_ref[rows, ca] + hba)
                ti = jnp.tanh(g_ref[rows, cx] + hbx)
                a = jnp.exp2(c2 * tr + c2)
                z = 1.0 - a * a
                bt = (z * lax.rsqrt(jnp.maximum(z, SQRT_TINY))) * ((ti + 1.0) * xb_ref[rows, cs])
                hl = a * hl + bt
                ac = a * ac
                g_ref[rows, ca] = hl
                g_ref[rows, cx] = ac
            if j + 1 < LRU_P // LRU_SCAN_RG:
                carry_ref[0, :, cs] = hl
                carry_ref[1, :, cs] = ac

    def segment_carries():
        last = slice(LRU_TS - SUBLANES, LRU_TS)
        for g in range(N_GATE_GROUPS):
            cs = slice(GATE_GROUP_W * g, GATE_GROUP_W * (g + 1))
            ca = slice(2 * GATE_GROUP_W * g, 2 * GATE_GROUP_W * g + GATE_GROUP_W)
            cx = slice(2 * GATE_GROUP_W * g + GATE_GROUP_W, 2 * GATE_GROUP_W * (g + 1))
            hprev = hc_ref[:, cs]
            acum, bcum = _scan8(g_ref[last, cx], g_ref[last, ca], rowid)
            hend = acum * hprev + bcum
            cin_ref[:, cs] = jnp.where(rowid == 0, hprev, pltpu.roll(hend, 1, axis=0))
            hc_ref[:, cs] = jnp.broadcast_to(hend[SUBLANES - 1:SUBLANES, :], hend.shape)

    def gate_out(k):
        for g in range(N_GATE_GROUPS):
            cs = slice(GATE_GROUP_W * g, GATE_GROUP_W * (g + 1))
            ca = slice(2 * GATE_GROUP_W * g, 2 * GATE_GROUP_W * g + GATE_GROUP_W)
            cx = slice(2 * GATE_GROUP_W * g + GATE_GROUP_W, 2 * GATE_GROUP_W * (g + 1))
            cin = cin_ref[:, cs]
            for ip in range(LRU_RG // 2):
                ys = []
                for d in range(2):
                    r0 = SUBLANES * (k * LRU_RG + 2 * ip + d)
                    rows = slice(r0, r0 + SUBLANES)
                    hseq = g_ref[rows, ca] + g_ref[rows, cx] * cin
                    ys.append(hseq * sg_ref[rows, cs])
                r0 = SUBLANES * (k * LRU_RG + 2 * ip)
                y_ref[r0:r0 + 2 * SUBLANES, cs] = jnp.concatenate(ys, axis=0).astype(BF16)

    def proj_out(k):
        o = jnp.dot(y_ref[LRU_BLK * k:LRU_BLK * (k + 1), :], wout_ref[...], preferred_element_type=F32)
        for i in range(LRU_RG):
            ov = o[SUBLANES * i:SUBLANES * (i + 1)]
            obuf[cur, k * LRU_RG + i] = xbuf[xs, k * LRU_RG + i] + ov * _rms_scale(ov) * gpost_ref[...]

    conv_header()
    conv(0)
    gates(0)
    conv(1)
    norm_in(xn, 0)
    norm_in(xn, 1)
    mxu_pieces = [lambda c=c: proj_in(c) for c in range(LRU_PROJ_PIECES)]
    mxu_pieces.insert(2, lambda: gates(1))
    n_slices = LRU_P // LRU_SCAN_RG
    for j in range(max(n_slices, len(mxu_pieces))):
        if j < len(mxu_pieces):
            mxu_pieces[j]()
        if j < n_slices:
            local_scan(j)
    segment_carries()
    for k in range(LRU_NB):
        gate_out(k)
        proj_out(k)

    for cp in _o_copies(obuf, o_hbm, out_sem, b, t, cur):
        cp.start()

    @pl.when(n == n_steps - 1)
    def _():
        for cp in _o_copies(obuf, o_hbm, out_sem, b, t, cur):
            cp.wait()
        for cp in _o_copies(obuf, o_hbm, out_sem, b, t, nxt):
            cp.wait()


def _ret_kernel(x_hbm, gpre_ref, gpost_ref, win_hbm, wout_hbm, cos_ref, sin_ref, dmask_ref, qdec_ref,
                kdec_ref, cdec_ref, o_ref, xbuf, in_sem, w_sem, win_ref, wout_ref, h_ref, u2_ref, y_ref,
                state_ref):
    ts = RET_TS
    nb_t = x_hbm.shape[1] // ts
    n_steps = x_hbm.shape[0] * nb_t
    t = pl.program_id(1)
    n = pl.program_id(0) * nb_t + t
    xs = n % 3
    xn = (n + 1) % 3
    u_bufs = (u2_ref.at[0], u2_ref.at[1])

    def x_copy(m, slot):
        return pltpu.make_async_copy(x_hbm.at[m // nb_t, pl.ds((m % nb_t) * ts, ts), :], xbuf.at[slot],
                                     in_sem.at[slot])

    def norm_in(slot):
        for r in range(ts // CHUNK):
            rows = slice(r * CHUNK, (r + 1) * CHUNK)
            x = xbuf[slot, rows, :]
            h_ref[rows, :] = (x * _rms_scale(x) * gpre_ref[...]).astype(BF16)

    def proj_head(hd):
        u_bufs[hd % 2][...] = jnp.dot(h_ref[...], win_ref[hd], preferred_element_type=F32)

    @pl.when(n == 0)
    def _():
        x_copy(n, xs).start()
        x_copy(n + 1, xn).start()
        w_copies = [pltpu.make_async_copy(win_hbm, win_ref, w_sem.at[0]),
                    pltpu.make_async_copy(wout_hbm, wout_ref, w_sem.at[1])]
        for cp in w_copies:
            cp.start()
        for cp in w_copies:
            cp.wait()
        x_copy(n, xs).wait()
        norm_in(xs)
        proj_head(0)

    @pl.when(n + 2 < n_steps)
    def _():
        x_copy(n + 2, (n + 2) % 3).start()

    @pl.when(n + 1 < n_steps)
    def _():
        x_copy(n + 1, xn).wait()

    @pl.when(t == 0)
    def _():
        state_ref[...] = jnp.zeros_like(state_ref)

    half = QK_HEAD // 2
    for hd in range(RET_HEADS):
        if hd + 1 < RET_HEADS:
            proj_head(hd + 1)
        else:
            norm_in(xn)
        u_ref = u_bufs[hd % 2]
        for c in range(ts // CHUNK):
            rows = slice(c * CHUNK, (c + 1) * CHUNK)
            cos = cos_ref[rows, :]
            sin = sin_ref[rows, :]
            q1 = u_ref[rows, 0:half]
            q2 = u_ref[rows, half:QK_HEAD]
            k1 = u_ref[rows, QK_HEAD:QK_HEAD + half]
            k2 = u_ref[rows, QK_HEAD + half:2 * QK_HEAD]
            qr1 = q1 * cos - q2 * sin
            qr2 = q2 * cos + q1 * sin
            kr1 = (k1 * cos - k2 * sin) * (QK_HEAD ** -0.5)
            kr2 = (k2 * cos + k1 * sin) * (QK_HEAD ** -0.5)
            qdec = qdec_ref[hd]
            kdec = kdec_ref[hd]
            q = jnp.concatenate([qr1, qr2], axis=-1).astype(BF16)
            qd = jnp.concatenate([qr1 * qdec, qr2 * qdec], axis=-1).astype(BF16)
            k = jnp.concatenate([kr1, kr2], axis=-1).astype(BF16)
            kd = jnp.concatenate([kr1 * kdec, kr2 * kdec], axis=-1).astype(BF16)
            v = u_ref[rows, 2 * QK_HEAD:2 * QK_HEAD + V_HEAD].astype(BF16)
            s = lax.dot_general(q, k, (((1,), (1,)), ((), ())), preferred_element_type=F32)
            s = (s * dmask_ref[hd]).astype(BF16)
            inner = jnp.dot(s, v, preferred_element_type=F32)
            st = state_ref[hd]
            cross = jnp.dot(qd, st.astype(BF16), preferred_element_type=F32)
            state_ref[hd] = st * cdec_ref[hd] + lax.dot_general(
                kd, v, (((0,), (0,)), ((), ())), preferred_element_type=F32)
            o = inner + cross
            mu = jnp.mean(o, axis=-1, keepdims=True)
            oc = o - mu
            var = jnp.mean(oc * oc, axis=-1, keepdims=True)
            gn = oc * lax.rsqrt(var + GN_EPS)
            gate = u_ref[rows, 2 * QK_HEAD + V_HEAD:HEAD_COLS]
            y_ref[rows, hd * V_HEAD:(hd + 1) * V_HEAD] = (gn * (gate * _sigmoid(gate))).astype(BF16)

    o = jnp.dot(y_ref[...], wout_ref[...], preferred_element_type=F32)
    proj_head(0)
    o_ref[0] = xbuf[xs] + o * _rms_scale(o) * gpost_ref[...]


def _const_spec(shape):
    nd = len(shape)
    return pl.BlockSpec(shape, lambda b, t: (0,) * nd, pipeline_mode=pl.Buffered(1))


def _rows8(v):
    return jnp.broadcast_to(v[None, :], (SUBLANES, v.shape[0]))


def _lru_layer(x, gpre, gpost, w_in, conv_w, conv_b, wg, b_a, b_x, lam, w_out):
    bsz, s, _ = x.shape
    any_spec = pl.BlockSpec(memory_space=pl.ANY)
    return pl.pallas_call(
        _lru_kernel,
        grid=(bsz, s // LRU_TS),
        in_specs=[any_spec, _const_spec((SUBLANES, D_MODEL)), _const_spec((SUBLANES, D_MODEL)),
                  any_spec, _const_spec((CONV_W * SUBLANES, D_RNN)), _const_spec((SUBLANES, D_RNN)),
                  any_spec, _const_spec((SUBLANES, D_RNN)), _const_spec((SUBLANES, D_RNN)),
                  _const_spec((SUBLANES, D_RNN)), any_spec],
        out_specs=any_spec,
        out_shape=jax.ShapeDtypeStruct(x.shape, x.dtype),
        scratch_shapes=[
            pltpu.VMEM((3, LRU_P, SUBLANES, D_MODEL), F32),
            pltpu.VMEM((2, LRU_P, SUBLANES, D_MODEL), F32),
            pltpu.SemaphoreType.DMA((3,)),
            pltpu.SemaphoreType.DMA((2,)),
            pltpu.SemaphoreType.DMA((3,)),
            pltpu.VMEM((D_MODEL, 2 * D_RNN), BF16),
            pltpu.VMEM((N_GATE_GROUPS, GATE_GROUP_W, 2 * GATE_GROUP_W), BF16),
            pltpu.VMEM((D_RNN, D_MODEL), BF16),
            pltpu.VMEM((LRU_TS, D_MODEL), BF16),
            pltpu.VMEM((LRU_TS + (CONV_W - 1) * SUBLANES, D_RNN), F32),
            pltpu.VMEM((LRU_TS, D_RNN), F32),
            pltpu.VMEM((LRU_TS, D_RNN), F32),
            pltpu.VMEM((LRU_TS, D_RNN), F32),
            pltpu.VMEM((LRU_TS, D_RNN), BF16),
            pltpu.VMEM((LRU_TS, 2 * D_RNN), F32),
            pltpu.VMEM((LRU_TS, D_RNN), BF16),
            pltpu.VMEM(((CONV_W - 1) * SUBLANES, D_RNN), F32),
            pltpu.VMEM((SUBLANES, D_RNN), F32),
            pltpu.VMEM((SUBLANES, D_RNN), F32),
            pltpu.VMEM((SUBLANES, D_RNN), F32),
            pltpu.VMEM((2, SUBLANES, D_RNN), F32),
        ],
        compiler_params=pltpu.CompilerParams(
            dimension_semantics=("arbitrary", "arbitrary"), vmem_limit_bytes=VMEM_LIMIT),
        name="lru_layer",
    )(x, _rows8(gpre), _rows8(gpost), w_in, jnp.repeat(0.5 * conv_w, SUBLANES, axis=0), _rows8(0.5 * conv_b),
      wg, _rows8(0.5 * b_a), _rows8(0.5 * b_x), _rows8(lam), w_out)


def _ret_layer(x, gpre, gpost, w_in_heads, w_out, cos, sin, dmask, qdec, kdec, cdec):
    bsz, s, _ = x.shape
    ts = RET_TS
    x_spec = pl.BlockSpec((1, ts, D_MODEL), lambda b, t: (b, t, 0))
    rope_spec = pl.BlockSpec((ts, QK_HEAD // 2), lambda b, t: (t, 0))
    return pl.pallas_call(
        _ret_kernel,
        grid=(bsz, s // ts),
        in_specs=[pl.BlockSpec(memory_space=pl.ANY), _const_spec((1, D_MODEL)), _const_spec((1, D_MODEL)),
                  pl.BlockSpec(memory_space=pl.ANY), pl.BlockSpec(memory_space=pl.ANY),
                  rope_spec, rope_spec,
                  _const_spec((RET_HEADS, CHUNK, CHUNK)), _const_spec((RET_HEADS, CHUNK, QK_HEAD // 2)),
                  _const_spec((RET_HEADS, CHUNK, QK_HEAD // 2)),
                  pl.BlockSpec(memory_space=pltpu.SMEM)],
        out_specs=x_spec,
        out_shape=jax.ShapeDtypeStruct(x.shape, x.dtype),
        scratch_shapes=[
            pltpu.VMEM((3, ts, D_MODEL), F32),
            pltpu.SemaphoreType.DMA((3,)),
            pltpu.SemaphoreType.DMA((2,)),
            pltpu.VMEM((RET_HEADS, D_MODEL, HEAD_COLS), BF16),
            pltpu.VMEM((V_DIM, D_MODEL), BF16),
            pltpu.VMEM((ts, D_MODEL), BF16),
            pltpu.VMEM((2, ts, HEAD_COLS), F32),
            pltpu.VMEM((ts, V_DIM), BF16),
            pltpu.VMEM((RET_HEADS, QK_HEAD, V_HEAD), F32),
        ],
        compiler_params=pltpu.CompilerParams(
            dimension_semantics=("arbitrary", "arbitrary"), vmem_limit_bytes=VMEM_LIMIT),
        name="ret_layer",
    )(x, gpre, gpost, w_in_heads, w_out, cos, sin, dmask, qdec, kdec, cdec)


def _gate_group_weights(w_a, w_x):
    def bd(w):
        w = w.reshape(N_GATE_GROUPS, GATE_GROUP_BLOCKS, BLOCK_W, BLOCK_W)
        eye = jnp.eye(GATE_GROUP_BLOCKS, dtype=w.dtype)
        full = w[:, :, :, None, :] * eye[None, :, None, :, None]
        return full.reshape(N_GATE_GROUPS, GATE_GROUP_W, GATE_GROUP_W)
    return jnp.concatenate([bd(w_a), bd(w_x)], axis=-1).astype(BF16)


def _ret_head_weights(w_in):
    q = w_in[:, :QK_DIM].reshape(D_MODEL, RET_HEADS, QK_HEAD)
    k = w_in[:, QK_DIM:2 * QK_DIM].reshape(D_MODEL, RET_HEADS, QK_HEAD)
    v = w_in[:, 2 * QK_DIM:2 * QK_DIM + V_DIM].reshape(D_MODEL, RET_HEADS, V_HEAD)
    g = w_in[:, 2 * QK_DIM + V_DIM:].reshape(D_MODEL, RET_HEADS, V_HEAD)
    return jnp.concatenate([q, k, v, g], axis=-1).transpose(1, 0, 2).astype(BF16)


def _retention_tables(s):
    pos = jnp.arange(s, dtype=F32)
    inv_freq = ROPE_BASE ** (-jnp.arange(0, QK_HEAD, 2, dtype=F32) / QK_HEAD)
    ang = pos[:, None] * inv_freq[None, :]
    cos, sin = jnp.cos(ang), jnp.sin(ang)
    log_g = jnp.log(1.0 - jnp.exp2(-5.0 - jnp.arange(RET_HEADS, dtype=F32)))
    idx = jnp.arange(CHUNK, dtype=F32)
    diff = idx[:, None] - idx[None, :]
    dmask = jnp.where(diff[None] >= 0, jnp.exp(jnp.maximum(diff, 0.0)[None] * log_g[:, None, None]), 0.0)
    q_decay = jnp.exp((idx[None, :] + 1.0) * log_g[:, None])[..., None]
    k_decay = jnp.exp((CHUNK - 1.0 - idx[None, :]) * log_g[:, None])[..., None]
    qdec = jnp.broadcast_to(q_decay, (RET_HEADS, CHUNK, QK_HEAD // 2))
    kdec = jnp.broadcast_to(k_decay, (RET_HEADS, CHUNK, QK_HEAD // 2))
    cdec = jnp.exp(CHUNK * log_g)
    return cos, sin, dmask, qdec, kdec, cdec


def kernel(x, norm_pre, norm_post, lru_w_in, lru_conv_w, lru_conv_b, lru_w_a, lru_b_a, lru_w_x, lru_b_x,
           lru_lambda, lru_w_out, ret_w_in, ret_w_out):
    s = x.shape[1]
    tables = _retention_tables(s)
    for layer in range(DEPTH):
        j = layer // 2
        if layer % 2 == 0:
            w_in = jnp.concatenate([lru_w_in[j][:, :D_RNN], 0.5 * lru_w_in[j][:, D_RNN:]], axis=1).astype(BF16)
            x = _lru_layer(x, norm_pre[layer], norm_post[layer], w_in, lru_conv_w[j],
                           lru_conv_b[j], _gate_group_weights(lru_w_a[j], lru_w_x[j]), lru_b_a[j],
                           lru_b_x[j], lru_lambda[j], lru_w_out[j].astype(BF16))
        else:
            x = _ret_layer(x, norm_pre[layer][None, :], norm_post[layer][None, :],
                           _ret_head_weights(ret_w_in[j]), ret_w_out[j].astype(BF16), *tables)
    return x
```

```python
import jax
import jax.numpy as jnp
from jax import lax
from jax.experimental import pallas as pl
from jax.experimental.pallas import tpu as pltpu

D_MODEL = 1024
DEPTH = 4
D_RNN = 1536
N_GATE_BLOCKS = 16
BLOCK_W = 96
CONV_W = 4
C_RG = 8.0
RET_HEADS = 4
QK_HEAD = 256
V_HEAD = 512
QK_DIM = RET_HEADS * QK_HEAD
V_DIM = RET_HEADS * V_HEAD
CHUNK = 256
ROPE_BASE = 10000.0
RMS_EPS = 1e-6
GN_EPS = 1e-5

GATE_GROUP_BLOCKS = 4
GATE_GROUP_W = GATE_GROUP_BLOCKS * BLOCK_W
N_GATE_GROUPS = N_GATE_BLOCKS // GATE_GROUP_BLOCKS
HEAD_COLS = 2 * QK_HEAD + 2 * V_HEAD
SUBLANES = 8

LRU_TS = 512
LRU_P = LRU_TS // SUBLANES
LRU_BLK = 256
LRU_RG = LRU_BLK // SUBLANES
LRU_NB = LRU_TS // LRU_BLK
LRU_SCAN_RG = 8
LRU_PROJ_PIECES = 6
LRU_PROJ_W = 2 * D_RNN // LRU_PROJ_PIECES
RET_TS = 512
VMEM_LIMIT = 56 * 1024 * 1024

F32 = jnp.float32
BF16 = jnp.bfloat16
LOG2_E = 1.4426950408889634
SQRT_TINY = 1e-30


def _sigmoid(v):
    return 0.5 * jnp.tanh(0.5 * v) + 0.5


def _rms_scale(v):
    return lax.rsqrt(jnp.mean(v * v, axis=-1, keepdims=True) + RMS_EPS)


def _shift_rows(v, d, fill, rowid):
    return jnp.where(rowid >= d, pltpu.roll(v, d, axis=0), fill)


def _scan8(a, b, rowid):
    for d in (1, 2, 4):
        b = a * _shift_rows(b, d, 0.0, rowid) + b
        a = a * _shift_rows(a, d, 1.0, rowid)
    return a, b


def _x_copies(x_hbm, xbuf, sem, b, t, slot):
    return [pltpu.make_async_copy(x_hbm.at[b, pl.ds(t * LRU_TS + s * LRU_P, LRU_P), :],
                                  xbuf.at[slot, :, s, :], sem.at[slot]) for s in range(SUBLANES)]


def _o_copies(obuf, o_hbm, sem, b, t, slot):
    return [pltpu.make_async_copy(obuf.at[slot, :, s, :],
                                  o_hbm.at[b, pl.ds(t * LRU_TS + s * LRU_P, LRU_P), :], sem.at[slot])
            for s in range(SUBLANES)]


def _lru_kernel(x_hbm, gpre_ref, gpost_ref, win_hbm, convw_ref, convb_ref, wg_hbm, hba_ref, hbx_ref,
                lam_ref, wout_hbm, o_hbm, xbuf, obuf, in_sem, out_sem, w_sem, win_ref, wg_ref, wout_ref,
                h16_ref, xp_ref, gate_ref, sg_ref, xb_ref, xb16_ref, g_ref, y_ref, tail_ref, hc_ref, cin_ref,
                c2_ref, carry_ref, dep_ref):
    nb_t = x_hbm.shape[1] // LRU_TS
    n_steps = x_hbm.shape[0] * nb_t
    b = pl.program_id(0)
    t = pl.program_id(1)
    n = b * nb_t + t
    cur = n % 2
    nxt = 1 - cur
    xs = n % 3
    xn = (n + 1) % 3

    def x_copies(m, slot):
        return _x_copies(x_hbm, xbuf, in_sem, m // nb_t, m % nb_t, slot)

    hdr = (CONV_W - 1) * SUBLANES

    def norm_in(xslot, k):
        for ip in range(LRU_RG // 2):
            i0 = k * LRU_RG + 2 * ip
            hs = []
            for d in range(2):
                v = xbuf[xslot, i0 + d]
                hs.append(v * _rms_scale(v) * gpre_ref[...])
            h16_ref[SUBLANES * i0:SUBLANES * (i0 + 2), :] = jnp.concatenate(hs, axis=0).astype(BF16)

    def proj_in(c):
        cols = slice(LRU_PROJ_W * c, LRU_PROJ_W * (c + 1))
        res = jnp.dot(h16_ref[...], win_ref[:, cols], preferred_element_type=F32)
        if c < LRU_PROJ_PIECES // 2:
            xp_ref[hdr:, cols] = res
        else:
            gate_ref[:, LRU_PROJ_W * c - D_RNN:LRU_PROJ_W * (c + 1) - D_RNN] = res
        bits = lax.bitcast_convert_type(res[LRU_TS - SUBLANES:, :GATE_GROUP_W], jnp.uint32)
        zero = lax.shift_right_logical(lax.shift_right_logical(bits, jnp.uint32(16)), jnp.uint32(16))
        dep_ref[c] = lax.bitcast_convert_type(zero, F32)

    @pl.when(n == 0)
    def _():
        for cp in x_copies(n, xs):
            cp.start()
        for cp in x_copies(n + 1, xn):
            cp.start()
        w_copies = [pltpu.make_async_copy(src, dst, w_sem.at[i]) for i, (src, dst) in enumerate(
            ((win_hbm, win_ref), (wg_hbm, wg_ref), (wout_hbm, wout_ref)))]
        for cp in w_copies:
            cp.start()
        for cp in w_copies:
            cp.wait()
        for r in range(0, D_MODEL, LRU_BLK):
            win_ref[r:r + LRU_BLK, D_RNN:] = win_ref[r:r + LRU_BLK, D_RNN:] * 0.5
        for cp in x_copies(n, xs):
            cp.wait()
        for k in range(LRU_NB):
            norm_in(xs, k)
        for c in range(LRU_PROJ_PIECES):
            proj_in(c)

    @pl.when(n + 2 < n_steps)
    def _():
        for cp in x_copies(n + 2, (n + 2) % 3):
            cp.start()

    @pl.when(n >= 2)
    def _():
        for cp in _o_copies(obuf, o_hbm, out_sem, b, t, cur):
            cp.wait()

    @pl.when(t == 0)
    def _():
        tail_ref[...] = jnp.zeros_like(tail_ref)
        hc_ref[...] = jnp.zeros_like(hc_ref)

    @pl.when(n + 1 < n_steps)
    def _():
        for cp in x_copies(n + 1, xn):
            cp.wait()

    lam = lam_ref[...]
    c2_ref[...] = (-0.5 * C_RG * LOG2_E) * (jnp.maximum(-lam, 0.0) + jnp.log1p(jnp.exp(-jnp.abs(lam))))
    rowid = lax.broadcasted_iota(jnp.int32, (SUBLANES, GATE_GROUP_W), 0)
    rowid_full = lax.broadcasted_iota(jnp.int32, (SUBLANES, D_RNN), 0)

    def conv_header():
        for g in range(CONV_W - 1):
            rows = slice(SUBLANES * g, SUBLANES * (g + 1))
            src = hdr + SUBLANES * (LRU_P - (CONV_W - 1) + g)
            cur_rows = xp_ref[src:src + SUBLANES, :]
            merged = jnp.where(rowid_full == SUBLANES - 1, tail_ref[rows, :], cur_rows)
            xp_ref[rows, :] = pltpu.roll(merged, 1, axis=0)
            tail_ref[rows, :] = cur_rows

    def conv(k):
        for ip in range(LRU_RG // 2):
            i0 = k * LRU_RG + 2 * ip
            outs = []
            for d in range(2):
                r0 = SUBLANES * (i0 + d)
                acc = convb_ref[...]
                for j in range(CONV_W):
                    acc = acc + convw_ref[SUBLANES * j:SUBLANES * (j + 1), :] * xp_ref[
                        r0 + SUBLANES * j:r0 + SUBLANES * (j + 1), :]
                xb_ref[r0:r0 + SUBLANES, :] = acc
                outs.append(acc)
                hg = gate_ref[r0:r0 + SUBLANES, :]
                sg_ref[r0:r0 + SUBLANES, :] = hg * jnp.tanh(hg) + hg
            xb16_ref[SUBLANES * i0:SUBLANES * (i0 + 2), :] = jnp.concatenate(outs, axis=0).astype(BF16)

    def gates(k):
        rows = slice(LRU_BLK * k, LRU_BLK * (k + 1))
        for g in range(N_GATE_GROUPS):
            g_ref[rows, 2 * GATE_GROUP_W * g:2 * GATE_GROUP_W * (g + 1)] = jnp.dot(
                xb16_ref[rows, GATE_GROUP_W * g:GATE_GROUP_W * (g + 1)], wg_ref[g],
                preferred_element_type=F32)

    def local_scan(j):
        for g in range(N_GATE_GROUPS):
            cs = slice(GATE_GROUP_W * g, GATE_GROUP_W * (g + 1))
            ca = slice(2 * GATE_GROUP_W * g, 2 * GATE_GROUP_W * g + GATE_GROUP_W)
            cx = slice(2 * GATE_GROUP_W * g + GATE_GROUP_W, 2 * GATE_GROUP_W * (g + 1))
            if j == 0:
                hl = jnp.zeros((SUBLANES, GATE_GROUP_W), F32)
                ac = jnp.ones((SUBLANES, GATE_GROUP_W), F32)
            else:
                hl = carry_ref[0, :, cs]
                ac = carry_ref[1, :, cs]
            c2 = c2_ref[:, cs]
            if j % 2 == 0 and j // 2 < LRU_PROJ_PIECES:
                c2 = c2 + dep_ref[j // 2]
            hba = hba_ref[:, cs]
            hbx = hbx_ref[:, cs]
            for i in range(LRU_SCAN_RG):
                rows = slice(SUBLANES * (j * LRU_SCAN_RG + i), SUBLANES * (j * LRU_SCAN_RG + i + 1))
                tr = jnp.tanh(g_ref[rows, ca] + hba)
                ti = jnp.tanh(g_ref[rows, cx] + hbx)
                a = jnp.exp2(c2 * tr + c2)
                z = 1.0 - a * a
                bt = (z * lax.rsqrt(jnp.maximum(z, SQRT_TINY))) * ((ti + 1.0) * xb_ref[rows, cs])
                hl = a * hl + bt
                ac = a * ac
                g_ref[rows, ca] = hl
                g_ref[rows, cx] = ac
            if j + 1 < LRU_P // LRU_SCAN_RG:
                carry_ref[0, :, cs] = hl
                carry_ref[1, :, cs] = ac

    def segment_carries():
        last = slice(LRU_TS - SUBLANES, LRU_TS)
        for g in range(N_GATE_GROUPS):
            cs = slice(GATE_GROUP_W * g, GATE_GROUP_W * (g + 1))
            ca = slice(2 * GATE_GROUP_W * g, 2 * GATE_GROUP_W * g + GATE_GROUP_W)
            cx = slice(2 * GATE_GROUP_W * g + GATE_GROUP_W, 2 * GATE_GROUP_W * (g + 1))
            hprev = hc_ref[:, cs]
            acum, bcum = _scan8(g_ref[last, cx], g_ref[last, ca], rowid)
            hend = acum * hprev + bcum
            cin_ref[:, cs] = jnp.where(rowid == 0, hprev, pltpu.roll(hend, 1, axis=0))
            hc_ref[:, cs] = jnp.broadcast_to(hend[SUBLANES - 1:SUBLANES, :], hend.shape)

    def gate_out(k):
        for g in range(N_GATE_GROUPS):
            cs = slice(GATE_GROUP_W * g, GATE_GROUP_W * (g + 1))
            ca = slice(2 * GATE_GROUP_W * g, 2 * GATE_GROUP_W * g + GATE_GROUP_W)
            cx = slice(2 * GATE_GROUP_W * g + GATE_GROUP_W, 2 * GATE_GROUP_W * (g + 1))
            cin = cin_ref[:, cs] + dep_ref[LRU_PROJ_PIECES - LRU_NB + k]
            for ip in range(LRU_RG // 2):
                ys = []
                for d in range(2):
                    r0 = SUBLANES * (k * LRU_RG + 2 * ip + d)
                    rows = slice(r0, r0 + SUBLANES)
                    hseq = g_ref[rows, ca] + g_ref[rows, cx] * cin
                    ys.append(hseq * sg_ref[rows, cs])
                r0 = SUBLANES * (k * LRU_RG + 2 * ip)
                y_ref[r0:r0 + 2 * SUBLANES, cs] = jnp.concatenate(ys, axis=0).astype(BF16)

    def proj_out(k):
        o = jnp.dot(y_ref[LRU_BLK * k:LRU_BLK * (k + 1), :], wout_ref[...], preferred_element_type=F32)
        for i in range(LRU_RG):
            ov = o[SUBLANES * i:SUBLANES * (i + 1)]
            obuf[cur, k * LRU_RG + i] = xbuf[xs, k * LRU_RG + i] + ov * _rms_scale(ov) * gpost_ref[...]

    n_slices = LRU_P // LRU_SCAN_RG
    conv_header()
    conv(0)
    gates(0)
    conv(1)
    norm_in(xn, 0)
    norm_in(xn, 1)
    gates(1)
    for j in range(n_slices):
        if j % 2 == 0 and j // 2 < LRU_PROJ_PIECES - LRU_NB:
            proj_in(j // 2)
        local_scan(j)
    segment_carries()
    for k in range(LRU_NB):
        proj_in(LRU_PROJ_PIECES - LRU_NB + k)
        gate_out(k)
    for k in range(LRU_NB):
        proj_out(k)

    for cp in _o_copies(obuf, o_hbm, out_sem, b, t, cur):
        cp.start()

    @pl.when(n == n_steps - 1)
    def _():
        for cp in _o_copies(obuf, o_hbm, out_sem, b, t, cur):
            cp.wait()
        for cp in _o_copies(obuf, o_hbm, out_sem, b, t, nxt):
            cp.wait()


def _ret_kernel(x_hbm, gpre_ref, gpost_ref, win_hbm, wout_hbm, cos_ref, sin_ref, dmask_ref, qdec_ref,
                kdec_ref, cdec_ref, o_ref, xbuf, in_sem, w_sem, win_ref, wout_ref, h_ref, u2_ref, y_ref,
                state_ref):
    ts = RET_TS
    nb_t = x_hbm.shape[1] // ts
    n_steps = x_hbm.shape[0] * nb_t
    t = pl.program_id(1)
    n = pl.program_id(0) * nb_t + t
    xs = n % 3
    xn = (n + 1) % 3
    u_bufs = (u2_ref.at[0], u2_ref.at[1])

    def x_copy(m, slot):
        return pltpu.make_async_copy(x_hbm.at[m // nb_t, pl.ds((m % nb_t) * ts, ts), :], xbuf.at[slot],
                                     in_sem.at[slot])

    def norm_in(slot):
        for r in range(ts // CHUNK):
            rows = slice(r * CHUNK, (r + 1) * CHUNK)
            x = xbuf[slot, rows, :]
            h_ref[rows, :] = (x * _rms_scale(x) * gpre_ref[...]).astype(BF16)

    def proj_head(hd):
        u_bufs[hd % 2][...] = jnp.dot(h_ref[...], win_ref[hd], preferred_element_type=F32)

    @pl.when(n == 0)
    def _():
        x_copy(n, xs).start()
        x_copy(n + 1, xn).start()
        w_copies = [pltpu.make_async_copy(wout_hbm, wout_ref, w_sem.at[1])]
        for hd in range(RET_HEADS):
            dst = 0
            for src0, w in ((0, QK_HEAD), (QK_DIM, QK_HEAD), (2 * QK_DIM, V_HEAD), (2 * QK_DIM + V_DIM, V_HEAD)):
                w_copies.append(pltpu.make_async_copy(
                    win_hbm.at[:, pl.ds(src0 + hd * w, w)], win_ref.at[hd, :, pl.ds(dst, w)], w_sem.at[0]))
                dst += w
        for cp in w_copies:
            cp.start()
        for cp in w_copies:
            cp.wait()
        x_copy(n, xs).wait()
        norm_in(xs)
        proj_head(0)

    @pl.when(n + 2 < n_steps)
    def _():
        x_copy(n + 2, (n + 2) % 3).start()

    @pl.when(n + 1 < n_steps)
    def _():
        x_copy(n + 1, xn).wait()

    @pl.when(t == 0)
    def _():
        state_ref[...] = jnp.zeros_like(state_ref)

    half = QK_HEAD // 2
    for hd in range(RET_HEADS):
        if hd + 1 < RET_HEADS:
            proj_head(hd + 1)
        else:
            norm_in(xn)
        u_ref = u_bufs[hd % 2]
        for c in range(ts // CHUNK):
            rows = slice(c * CHUNK, (c + 1) * CHUNK)
            cos = cos_ref[rows, :]
            sin = sin_ref[rows, :]
            q1 = u_ref[rows, 0:half]
            q2 = u_ref[rows, half:QK_HEAD]
            k1 = u_ref[rows, QK_HEAD:QK_HEAD + half]
            k2 = u_ref[rows, QK_HEAD + half:2 * QK_HEAD]
            qr1 = q1 * cos - q2 * sin
            qr2 = q2 * cos + q1 * sin
            kr1 = (k1 * cos - k2 * sin) * (QK_HEAD ** -0.5)
            kr2 = (k2 * cos + k1 * sin) * (QK_HEAD ** -0.5)
            qdec = qdec_ref[hd]
            kdec = kdec_ref[hd]
            q = jnp.concatenate([qr1, qr2], axis=-1).astype(BF16)
            qd = jnp.concatenate([qr1 * qdec, qr2 * qdec], axis=-1).astype(BF16)
            k = jnp.concatenate([kr1, kr2], axis=-1).astype(BF16)
            kd = jnp.concatenate([kr1 * kdec, kr2 * kdec], axis=-1).astype(BF16)
            v = u_ref[rows, 2 * QK_HEAD:2 * QK_HEAD + V_HEAD].astype(BF16)
            s = lax.dot_general(q, k, (((1,), (1,)), ((), ())), preferred_element_type=F32)
            s = (s * dmask_ref[hd]).astype(BF16)
            inner = jnp.dot(s, v, preferred_element_type=F32)
            st = state_ref[hd]
            cross = jnp.dot(qd, st.astype(BF16), preferred_element_type=F32)
            state_ref[hd] = st * cdec_ref[hd] + lax.dot_general(
                kd, v, (((0,), (0,)), ((), ())), preferred_element_type=F32)
            o = inner + cross
            mu = jnp.mean(o, axis=-1, keepdims=True)
            oc = o - mu
            var = jnp.mean(oc * oc, axis=-1, keepdims=True)
            gn = oc * lax.rsqrt(var + GN_EPS)
            gate = u_ref[rows, 2 * QK_HEAD + V_HEAD:HEAD_COLS]
            y_ref[rows, hd * V_HEAD:(hd + 1) * V_HEAD] = (gn * (gate * _sigmoid(gate))).astype(BF16)

    o = jnp.dot(y_ref[...], wout_ref[...], preferred_element_type=F32)
    proj_head(0)
    o_ref[0] = xbuf[xs] + o * _rms_scale(o) * gpost_ref[...]


def _const_spec(shape):
    nd = len(shape)
    return pl.BlockSpec(shape, lambda b, t: (0,) * nd, pipeline_mode=pl.Buffered(1))


def _rows8(v):
    return jnp.broadcast_to(v[None, :], (SUBLANES, v.shape[0]))


def _lru_layer(x, gpre, gpost, w_in, conv_w, conv_b, wg, b_a, b_x, lam, w_out):
    bsz, s, _ = x.shape
    any_spec = pl.BlockSpec(memory_space=pl.ANY)
    return pl.pallas_call(
        _lru_kernel,
        grid=(bsz, s // LRU_TS),
        in_specs=[any_spec, _const_spec((SUBLANES, D_MODEL)), _const_spec((SUBLANES, D_MODEL)),
                  any_spec, _const_spec((CONV_W * SUBLANES, D_RNN)), _const_spec((SUBLANES, D_RNN)),
                  any_spec, _const_spec((SUBLANES, D_RNN)), _const_spec((SUBLANES, D_RNN)),
                  _const_spec((SUBLANES, D_RNN)), any_spec],
        out_specs=any_spec,
        out_shape=jax.ShapeDtypeStruct(x.shape, x.dtype),
        scratch_shapes=[
            pltpu.VMEM((3, LRU_P, SUBLANES, D_MODEL), F32),
            pltpu.VMEM((2, LRU_P, SUBLANES, D_MODEL), F32),
            pltpu.SemaphoreType.DMA((3,)),
            pltpu.SemaphoreType.DMA((2,)),
            pltpu.SemaphoreType.DMA((3,)),
            pltpu.VMEM((D_MODEL, 2 * D_RNN), BF16),
            pltpu.VMEM((N_GATE_GROUPS, GATE_GROUP_W, 2 * GATE_GROUP_W), BF16),
            pltpu.VMEM((D_RNN, D_MODEL), BF16),
            pltpu.VMEM((LRU_TS, D_MODEL), BF16),
            pltpu.VMEM((LRU_TS + (CONV_W - 1) * SUBLANES, D_RNN), F32),
            pltpu.VMEM((LRU_TS, D_RNN), F32),
            pltpu.VMEM((LRU_TS, D_RNN), F32),
            pltpu.VMEM((LRU_TS, D_RNN), F32),
            pltpu.VMEM((LRU_TS, D_RNN), BF16),
            pltpu.VMEM((LRU_TS, 2 * D_RNN), F32),
            pltpu.VMEM((LRU_TS, D_RNN), BF16),
            pltpu.VMEM(((CONV_W - 1) * SUBLANES, D_RNN), F32),
            pltpu.VMEM((SUBLANES, D_RNN), F32),
            pltpu.VMEM((SUBLANES, D_RNN), F32),
            pltpu.VMEM((SUBLANES, D_RNN), F32),
            pltpu.VMEM((2, SUBLANES, D_RNN), F32),
            pltpu.VMEM((LRU_PROJ_PIECES, SUBLANES, GATE_GROUP_W), F32),
        ],
        compiler_params=pltpu.CompilerParams(
            dimension_semantics=("arbitrary", "arbitrary"), vmem_limit_bytes=VMEM_LIMIT),
        name="lru_layer",
    )(x, _rows8(gpre), _rows8(gpost), w_in, jnp.repeat(0.5 * conv_w, SUBLANES, axis=0), _rows8(0.5 * conv_b),
      wg, _rows8(0.5 * b_a), _rows8(0.5 * b_x), _rows8(lam), w_out)


def _ret_layer(x, gpre, gpost, w_in_heads, w_out, cos, sin, dmask, qdec, kdec, cdec):
    bsz, s, _ = x.shape
    ts = RET_TS
    x_spec = pl.BlockSpec((1, ts, D_MODEL), lambda b, t: (b, t, 0))
    rope_spec = pl.BlockSpec((ts, QK_HEAD // 2), lambda b, t: (t, 0))
    return pl.pallas_call(
        _ret_kernel,
        grid=(bsz, s // ts),
        in_specs=[pl.BlockSpec(memory_space=pl.ANY), _const_spec((1, D_MODEL)), _const_spec((1, D_MODEL)),
                  pl.BlockSpec(memory_space=pl.ANY), pl.BlockSpec(memory_space=pl.ANY),
                  rope_spec, rope_spec,
                  _const_spec((RET_HEADS, CHUNK, CHUNK)), _const_spec((RET_HEADS, CHUNK, QK_HEAD // 2)),
                  _const_spec((RET_HEADS, CHUNK, QK_HEAD // 2)),
                  pl.BlockSpec(memory_space=pltpu.SMEM)],
        out_specs=x_spec,
        out_shape=jax.ShapeDtypeStruct(x.shape, x.dtype),
        scratch_shapes=[
            pltpu.VMEM((3, ts, D_MODEL), F32),
            pltpu.SemaphoreType.DMA((3,)),
            pltpu.SemaphoreType.DMA((2,)),
            pltpu.VMEM((RET_HEADS, D_MODEL, HEAD_COLS), BF16),
            pltpu.VMEM((V_DIM, D_MODEL), BF16),
            pltpu.VMEM((ts, D_MODEL), BF16),
            pltpu.VMEM((2, ts, HEAD_COLS), F32),
            pltpu.VMEM((ts, V_DIM), BF16),
            pltpu.VMEM((RET_HEADS, QK_HEAD, V_HEAD), F32),
        ],
        compiler_params=pltpu.CompilerParams(
            dimension_semantics=("arbitrary", "arbitrary"), vmem_limit_bytes=VMEM_LIMIT),
        name="ret_layer",
    )(x, gpre, gpost, w_in_heads, w_out, cos, sin, dmask, qdec, kdec, cdec)


def _gate_group_weights(w_a, w_x):
    def bd(w):
        w = w.reshape(N_GATE_GROUPS, GATE_GROUP_BLOCKS, BLOCK_W, BLOCK_W)
        eye = jnp.eye(GATE_GROUP_BLOCKS, dtype=w.dtype)
        full = w[:, :, :, None, :] * eye[None, :, None, :, None]
        return full.reshape(N_GATE_GROUPS, GATE_GROUP_W, GATE_GROUP_W)
    return jnp.concatenate([bd(w_a), bd(w_x)], axis=-1).astype(BF16)


def _retention_tables(s):
    pos = jnp.arange(s, dtype=F32)
    inv_freq = ROPE_BASE ** (-jnp.arange(0, QK_HEAD, 2, dtype=F32) / QK_HEAD)
    ang = pos[:, None] * inv_freq[None, :]
    cos, sin = jnp.cos(ang), jnp.sin(ang)
    log_g = jnp.log(1.0 - jnp.exp2(-5.0 - jnp.arange(RET_HEADS, dtype=F32)))
    idx = jnp.arange(CHUNK, dtype=F32)
    diff = idx[:, None] - idx[None, :]
    dmask = jnp.where(diff[None] >= 0, jnp.exp(jnp.maximum(diff, 0.0)[None] * log_g[:, None, None]), 0.0)
    q_decay = jnp.exp((idx[None, :] + 1.0) * log_g[:, None])[..., None]
    k_decay = jnp.exp((CHUNK - 1.0 - idx[None, :]) * log_g[:, None])[..., None]
    qdec = jnp.broadcast_to(q_decay, (RET_HEADS, CHUNK, QK_HEAD // 2))
    kdec = jnp.broadcast_to(k_decay, (RET_HEADS, CHUNK, QK_HEAD // 2))
    cdec = jnp.exp(CHUNK * log_g)
    return cos, sin, dmask, qdec, kdec, cdec


def kernel(x, norm_pre, norm_post, lru_w_in, lru_conv_w, lru_conv_b, lru_w_a, lru_b_a, lru_w_x, lru_b_x,
           lru_lambda, lru_w_out, ret_w_in, ret_w_out):
    s = x.shape[1]
    tables = _retention_tables(s)
    for layer in range(DEPTH):
        j = layer // 2
        if layer % 2 == 0:
            x = _lru_layer(x, norm_pre[layer], norm_post[layer], lru_w_in[j].astype(BF16), lru_conv_w[j],
                           lru_conv_b[j], _gate_group_weights(lru_w_a[j], lru_w_x[j]), lru_b_a[j],
                           lru_b_x[j], lru_lambda[j], lru_w_out[j].astype(BF16))
        else:
            x = _ret_layer(x, norm_pre[layer][None, :], norm_post[layer][None, :],
                           ret_w_in[j].astype(BF16), ret_w_out[j].astype(BF16), *tables)
    return x
```

```python
import functools

import jax
import jax.numpy as jnp
from jax import lax
from jax.experimental import pallas as pl
from jax.experimental.pallas import tpu as pltpu

D_MODEL = 1024
DEPTH = 4
D_RNN = 1536
N_GATE_BLOCKS = 16
BLOCK_W = 96
CONV_W = 4
C_RG = 8.0
RET_HEADS = 4
QK_HEAD = 256
V_HEAD = 512
QK_DIM = RET_HEADS * QK_HEAD
V_DIM = RET_HEADS * V_HEAD
CHUNK = 256
ROPE_BASE = 10000.0
RMS_EPS = 1e-6
GN_EPS = 1e-5

GATE_GROUP_BLOCKS = 4
GATE_GROUP_W = GATE_GROUP_BLOCKS * BLOCK_W
N_GATE_GROUPS = N_GATE_BLOCKS // GATE_GROUP_BLOCKS
HEAD_COLS = 2 * QK_HEAD + 2 * V_HEAD
SUBLANES = 8

LRU_TS = 512
LRU_P = LRU_TS // SUBLANES
LRU_BLK = 256
LRU_RG = LRU_BLK // SUBLANES
LRU_NB = LRU_TS // LRU_BLK
LRU_SCAN_RG = 8
LRU_PROJ_PIECES = 6
LRU_PROJ_W = 2 * D_RNN // LRU_PROJ_PIECES
RET_TS = 512
VMEM_LIMIT = 56 * 1024 * 1024

F32 = jnp.float32
BF16 = jnp.bfloat16
LOG2_E = 1.4426950408889634
SQRT_TINY = 1e-30


def _sigmoid(v):
    return 0.5 * jnp.tanh(0.5 * v) + 0.5


def _rms_scale(v):
    return lax.rsqrt(jnp.mean(v * v, axis=-1, keepdims=True) + RMS_EPS)


def _shift_rows(v, d, fill, rowid):
    return jnp.where(rowid >= d, pltpu.roll(v, d, axis=0), fill)


def _scan8(a, b, rowid):
    for d in (1, 2, 4):
        b = a * _shift_rows(b, d, 0.0, rowid) + b
        a = a * _shift_rows(a, d, 1.0, rowid)
    return a, b


def _x_copies(x_hbm, xbuf, sem, b, t, slot):
    return [pltpu.make_async_copy(x_hbm.at[b, pl.ds(t * LRU_TS + s * LRU_P, LRU_P), :],
                                  xbuf.at[slot, :, s, :], sem.at[slot]) for s in range(SUBLANES)]


def _o_copies(obuf, o_hbm, sem, b, t, slot):
    return [pltpu.make_async_copy(obuf.at[slot, :, s, :],
                                  o_hbm.at[b, pl.ds(t * LRU_TS + s * LRU_P, LRU_P), :], sem.at[slot])
            for s in range(SUBLANES)]


def _lru_kernel(wj, x_hbm, gpre_ref, gpost_ref, win_hbm, convw_ref, convb_ref, wg_hbm, hba_ref, hbx_ref,
                lam_ref, wout_hbm, o_hbm, xbuf, obuf, in_sem, out_sem, w_sem, win_ref, wg_ref, wout_ref,
                h16_ref, xp_ref, gate_ref, sg_ref, xb_ref, xb16_ref, g_ref, y_ref, tail_ref, hc_ref, cin_ref,
                c2_ref, carry_ref):
    nb_t = x_hbm.shape[1] // LRU_TS
    n_steps = x_hbm.shape[0] * nb_t
    b = pl.program_id(0)
    t = pl.program_id(1)
    n = b * nb_t + t
    cur = n % 2
    nxt = 1 - cur
    xs = n % 3
    xn = (n + 1) % 3

    def x_copies(m, slot):
        return _x_copies(x_hbm, xbuf, in_sem, m // nb_t, m % nb_t, slot)

    hdr = (CONV_W - 1) * SUBLANES

    def norm_in(xslot, k):
        for ip in range(LRU_RG // 2):
            i0 = k * LRU_RG + 2 * ip
            hs = []
            for d in range(2):
                v = xbuf[xslot, i0 + d]
                hs.append(v * _rms_scale(v) * gpre_ref[...])
            h16_ref[SUBLANES * i0:SUBLANES * (i0 + 2), :] = jnp.concatenate(hs, axis=0).astype(BF16)

    def proj_in(c):
        cols = slice(LRU_PROJ_W * c, LRU_PROJ_W * (c + 1))
        res = jnp.dot(h16_ref[...], win_ref[:, cols], preferred_element_type=F32)
        if c < LRU_PROJ_PIECES // 2:
            xp_ref[hdr:, cols] = res
        else:
            gate_ref[:, LRU_PROJ_W * c - D_RNN:LRU_PROJ_W * (c + 1) - D_RNN] = res

    @pl.when(n == 0)
    def _():
        for cp in x_copies(n, xs):
            cp.start()
        for cp in x_copies(n + 1, xn):
            cp.start()
        w_copies = [pltpu.make_async_copy(src, dst, w_sem.at[i]) for i, (src, dst) in enumerate(
            ((win_hbm.at[wj], win_ref), (wg_hbm.at[wj], wg_ref), (wout_hbm.at[wj], wout_ref)))]
        for cp in w_copies:
            cp.start()
        for cp in w_copies:
            cp.wait()
        for r in range(0, D_MODEL, LRU_BLK):
            win_ref[r:r + LRU_BLK, D_RNN:] = win_ref[r:r + LRU_BLK, D_RNN:] * 0.5
        for cp in x_copies(n, xs):
            cp.wait()
        for k in range(LRU_NB):
            norm_in(xs, k)
        for c in range(LRU_PROJ_PIECES):
            proj_in(c)

    @pl.when(n + 2 < n_steps)
    def _():
        for cp in x_copies(n + 2, (n + 2) % 3):
            cp.start()

    @pl.when(n >= 2)
    def _():
        for cp in _o_copies(obuf, o_hbm, out_sem, b, t, cur):
            cp.wait()

    @pl.when(t == 0)
    def _():
        tail_ref[...] = jnp.zeros_like(tail_ref)
        hc_ref[...] = jnp.zeros_like(hc_ref)

    @pl.when(n + 1 < n_steps)
    def _():
        for cp in x_copies(n + 1, xn):
            cp.wait()

    lam = lam_ref[...]
    c2_ref[...] = (-0.5 * C_RG * LOG2_E) * (jnp.maximum(-lam, 0.0) + jnp.log1p(jnp.exp(-jnp.abs(lam))))
    rowid = lax.broadcasted_iota(jnp.int32, (SUBLANES, GATE_GROUP_W), 0)
    rowid_full = lax.broadcasted_iota(jnp.int32, (SUBLANES, D_RNN), 0)

    def conv_header():
        for g in range(CONV_W - 1):
            rows = slice(SUBLANES * g, SUBLANES * (g + 1))
            src = hdr + SUBLANES * (LRU_P - (CONV_W - 1) + g)
            cur_rows = xp_ref[src:src + SUBLANES, :]
            merged = jnp.where(rowid_full == SUBLANES - 1, tail_ref[rows, :], cur_rows)
            xp_ref[rows, :] = pltpu.roll(merged, 1, axis=0)
            tail_ref[rows, :] = cur_rows

    def conv(k):
        for ip in range(LRU_RG // 2):
            i0 = k * LRU_RG + 2 * ip
            outs = []
            for d in range(2):
                r0 = SUBLANES * (i0 + d)
                acc = convb_ref[...]
                for j in range(CONV_W):
                    acc = acc + convw_ref[SUBLANES * j:SUBLANES * (j + 1), :] * xp_ref[
                        r0 + SUBLANES * j:r0 + SUBLANES * (j + 1), :]
                xb_ref[r0:r0 + SUBLANES, :] = acc
                outs.append(acc)
                hg = gate_ref[r0:r0 + SUBLANES, :]
                sg_ref[r0:r0 + SUBLANES, :] = hg * jnp.tanh(hg) + hg
            xb16_ref[SUBLANES * i0:SUBLANES * (i0 + 2), :] = jnp.concatenate(outs, axis=0).astype(BF16)

    def gates(k):
        rows = slice(LRU_BLK * k, LRU_BLK * (k + 1))
        for g in range(N_GATE_GROUPS):
            g_ref[rows, 2 * GATE_GROUP_W * g:2 * GATE_GROUP_W * (g + 1)] = jnp.dot(
                xb16_ref[rows, GATE_GROUP_W * g:GATE_GROUP_W * (g + 1)], wg_ref[g],
                preferred_element_type=F32)

    def local_scan(j):
        for g in range(N_GATE_GROUPS):
            cs = slice(GATE_GROUP_W * g, GATE_GROUP_W * (g + 1))
            ca = slice(2 * GATE_GROUP_W * g, 2 * GATE_GROUP_W * g + GATE_GROUP_W)
            cx = slice(2 * GATE_GROUP_W * g + GATE_GROUP_W, 2 * GATE_GROUP_W * (g + 1))
            if j == 0:
                hl = jnp.zeros((SUBLANES, GATE_GROUP_W), F32)
                ac = jnp.ones((SUBLANES, GATE_GROUP_W), F32)
            else:
                hl = carry_ref[0, :, cs]
                ac = carry_ref[1, :, cs]
            c2 = c2_ref[:, cs]
            hba = hba_ref[:, cs]
            hbx = hbx_ref[:, cs]
            for i in range(LRU_SCAN_RG):
                rows = slice(SUBLANES * (j * LRU_SCAN_RG + i), SUBLANES * (j * LRU_SCAN_RG + i + 1))
                tr = jnp.tanh(g_ref[rows, ca] + hba)
                ti = jnp.tanh(g_ref[rows, cx] + hbx)
                a = jnp.exp2(c2 * tr + c2)
                z = 1.0 - a * a
                bt = (z * lax.rsqrt(jnp.maximum(z, SQRT_TINY))) * ((ti + 1.0) * xb_ref[rows, cs])
                hl = a * hl + bt
                ac = a * ac
                g_ref[rows, ca] = hl
                g_ref[rows, cx] = ac
            if j + 1 < LRU_P // LRU_SCAN_RG:
                carry_ref[0, :, cs] = hl
                carry_ref[1, :, cs] = ac

    def segment_carries():
        last = slice(LRU_TS - SUBLANES, LRU_TS)
        for g in range(N_GATE_GROUPS):
            cs = slice(GATE_GROUP_W * g, GATE_GROUP_W * (g + 1))
            ca = slice(2 * GATE_GROUP_W * g, 2 * GATE_GROUP_W * g + GATE_GROUP_W)
            cx = slice(2 * GATE_GROUP_W * g + GATE_GROUP_W, 2 * GATE_GROUP_W * (g + 1))
            hprev = hc_ref[:, cs]
            acum, bcum = _scan8(g_ref[last, cx], g_ref[last, ca], rowid)
            hend = acum * hprev + bcum
            cin_ref[:, cs] = jnp.where(rowid == 0, hprev, pltpu.roll(hend, 1, axis=0))
            hc_ref[:, cs] = jnp.broadcast_to(hend[SUBLANES - 1:SUBLANES, :], hend.shape)

    def gate_out(k):
        for g in range(N_GATE_GROUPS):
            cs = slice(GATE_GROUP_W * g, GATE_GROUP_W * (g + 1))
            ca = slice(2 * GATE_GROUP_W * g, 2 * GATE_GROUP_W * g + GATE_GROUP_W)
            cx = slice(2 * GATE_GROUP_W * g + GATE_GROUP_W, 2 * GATE_GROUP_W * (g + 1))
            cin = cin_ref[:, cs]
            for ip in range(LRU_RG // 2):
                ys = []
                for d in range(2):
                    r0 = SUBLANES * (k * LRU_RG + 2 * ip + d)
                    rows = slice(r0, r0 + SUBLANES)
                    hseq = g_ref[rows, ca] + g_ref[rows, cx] * cin
                    ys.append(hseq * sg_ref[rows, cs])
                r0 = SUBLANES * (k * LRU_RG + 2 * ip)
                y_ref[r0:r0 + 2 * SUBLANES, cs] = jnp.concatenate(ys, axis=0).astype(BF16)

    def proj_out(k):
        o = jnp.dot(y_ref[LRU_BLK * k:LRU_BLK * (k + 1), :], wout_ref[...], preferred_element_type=F32)
        for i in range(LRU_RG):
            ov = o[SUBLANES * i:SUBLANES * (i + 1)]
            obuf[cur, k * LRU_RG + i] = xbuf[xs, k * LRU_RG + i] + ov * _rms_scale(ov) * gpost_ref[...]

    conv_header()
    conv(0)
    gates(0)
    conv(1)
    norm_in(xn, 0)
    norm_in(xn, 1)
    mxu_pieces = [lambda c=c: proj_in(c) for c in range(LRU_PROJ_PIECES)]
    mxu_pieces.insert(2, lambda: gates(1))
    n_slices = LRU_P // LRU_SCAN_RG
    for j in range(max(n_slices, len(mxu_pieces))):
        if j < len(mxu_pieces):
            mxu_pieces[j]()
        if j < n_slices:
            local_scan(j)
    segment_carries()
    for k in range(LRU_NB):
        gate_out(k)
        proj_out(k)

    for cp in _o_copies(obuf, o_hbm, out_sem, b, t, cur):
        cp.start()

    @pl.when(n == n_steps - 1)
    def _():
        for cp in _o_copies(obuf, o_hbm, out_sem, b, t, cur):
            cp.wait()
        for cp in _o_copies(obuf, o_hbm, out_sem, b, t, nxt):
            cp.wait()


def _ret_kernel(wj, x_hbm, gpre_ref, gpost_ref, win_hbm, wout_hbm, cos_ref, sin_ref, dmask_ref, qdec_ref,
                kdec_ref, cdec_ref, o_ref, xbuf, in_sem, w_sem, win_ref, wout_ref, h_ref, u2_ref, y_ref,
                state_ref):
    ts = RET_TS
    nb_t = x_hbm.shape[1] // ts
    n_steps = x_hbm.shape[0] * nb_t
    t = pl.program_id(1)
    n = pl.program_id(0) * nb_t + t
    xs = n % 3
    xn = (n + 1) % 3
    u_bufs = (u2_ref.at[0], u2_ref.at[1])

    def x_copy(m, slot):
        return pltpu.make_async_copy(x_hbm.at[m // nb_t, pl.ds((m % nb_t) * ts, ts), :], xbuf.at[slot],
                                     in_sem.at[slot])

    def norm_in(slot):
        for r in range(ts // CHUNK):
            rows = slice(r * CHUNK, (r + 1) * CHUNK)
            x = xbuf[slot, rows, :]
            h_ref[rows, :] = (x * _rms_scale(x) * gpre_ref[...]).astype(BF16)

    def proj_head(hd):
        u_bufs[hd % 2][...] = jnp.dot(h_ref[...], win_ref[hd], preferred_element_type=F32)

    @pl.when(n == 0)
    def _():
        x_copy(n, xs).start()
        x_copy(n + 1, xn).start()
        w_copies = [pltpu.make_async_copy(wout_hbm.at[wj], wout_ref, w_sem.at[1])]
        for hd in range(RET_HEADS):
            dst = 0
            for src0, w in ((0, QK_HEAD), (QK_DIM, QK_HEAD), (2 * QK_DIM, V_HEAD), (2 * QK_DIM + V_DIM, V_HEAD)):
                w_copies.append(pltpu.make_async_copy(
                    win_hbm.at[wj, :, pl.ds(src0 + hd * w, w)], win_ref.at[hd, :, pl.ds(dst, w)], w_sem.at[0]))
                dst += w
        for cp in w_copies:
            cp.start()
        for cp in w_copies:
            cp.wait()
        x_copy(n, xs).wait()
        norm_in(xs)
        proj_head(0)

    @pl.when(n + 2 < n_steps)
    def _():
        x_copy(n + 2, (n + 2) % 3).start()

    @pl.when(n + 1 < n_steps)
    def _():
        x_copy(n + 1, xn).wait()

    @pl.when(t == 0)
    def _():
        state_ref[...] = jnp.zeros_like(state_ref)

    half = QK_HEAD // 2
    for hd in range(RET_HEADS):
        if hd + 1 < RET_HEADS:
            proj_head(hd + 1)
        else:
            norm_in(xn)
        u_ref = u_bufs[hd % 2]
        for c in range(ts // CHUNK):
            rows = slice(c * CHUNK, (c + 1) * CHUNK)
            cos = cos_ref[rows, :]
            sin = sin_ref[rows, :]
            q1 = u_ref[rows, 0:half]
            q2 = u_ref[rows, half:QK_HEAD]
            k1 = u_ref[rows, QK_HEAD:QK_HEAD + half]
            k2 = u_ref[rows, QK_HEAD + half:2 * QK_HEAD]
            qr1 = q1 * cos - q2 * sin
            qr2 = q2 * cos + q1 * sin
            kr1 = (k1 * cos - k2 * sin) * (QK_HEAD ** -0.5)
            kr2 = (k2 * cos + k1 * sin) * (QK_HEAD ** -0.5)
            qdec = qdec_ref[hd]
            kdec = kdec_ref[hd]
            q = jnp.concatenate([qr1, qr2], axis=-1).astype(BF16)
            qd = jnp.concatenate([qr1 * qdec, qr2 * qdec], axis=-1).astype(BF16)
            k = jnp.concatenate([kr1, kr2], axis=-1).astype(BF16)
            kd = jnp.concatenate([kr1 * kdec, kr2 * kdec], axis=-1).astype(BF16)
            v = u_ref[rows, 2 * QK_HEAD:2 * QK_HEAD + V_HEAD].astype(BF16)
            s = lax.dot_general(q, k, (((1,), (1,)), ((), ())), preferred_element_type=F32)
            s = (s * dmask_ref[hd]).astype(BF16)
            inner = jnp.dot(s, v, preferred_element_type=F32)
            st = state_ref[hd]
            cross = jnp.dot(qd, st.astype(BF16), preferred_element_type=F32)
            state_ref[hd] = st * cdec_ref[hd] + lax.dot_general(
                kd, v, (((0,), (0,)), ((), ())), preferred_element_type=F32)
            o = inner + cross
            mu = jnp.mean(o, axis=-1, keepdims=True)
            oc = o - mu
            var = jnp.mean(oc * oc, axis=-1, keepdims=True)
            gn = oc * lax.rsqrt(var + GN_EPS)
            gate = u_ref[rows, 2 * QK_HEAD + V_HEAD:HEAD_COLS]
            y_ref[rows, hd * V_HEAD:(hd + 1) * V_HEAD] = (gn * (gate * _sigmoid(gate))).astype(BF16)

    o = jnp.dot(y_ref[...], wout_ref[...], preferred_element_type=F32)
    proj_head(0)
    o_ref[0] = xbuf[xs] + o * _rms_scale(o) * gpost_ref[...]


def _const_spec(shape):
    nd = len(shape)
    return pl.BlockSpec(shape, lambda b, t: (0,) * nd, pipeline_mode=pl.Buffered(1))


def _rows8(v):
    return jnp.broadcast_to(v[None, :], (SUBLANES, v.shape[0]))


def _lru_layer(wj, x, gpre, gpost, w_in, conv_w, conv_b, wg, b_a, b_x, lam, w_out):
    bsz, s, _ = x.shape
    any_spec = pl.BlockSpec(memory_space=pl.ANY)
    return pl.pallas_call(
        functools.partial(_lru_kernel, wj),
        grid=(bsz, s // LRU_TS),
        in_specs=[any_spec, _const_spec((SUBLANES, D_MODEL)), _const_spec((SUBLANES, D_MODEL)),
                  any_spec, _const_spec((CONV_W * SUBLANES, D_RNN)), _const_spec((SUBLANES, D_RNN)),
                  any_spec, _const_spec((SUBLANES, D_RNN)), _const_spec((SUBLANES, D_RNN)),
                  _const_spec((SUBLANES, D_RNN)), any_spec],
        out_specs=any_spec,
        out_shape=jax.ShapeDtypeStruct(x.shape, x.dtype),
        scratch_shapes=[
            pltpu.VMEM((3, LRU_P, SUBLANES, D_MODEL), F32),
            pltpu.VMEM((2, LRU_P, SUBLANES, D_MODEL), F32),
            pltpu.SemaphoreType.DMA((3,)),
            pltpu.SemaphoreType.DMA((2,)),
            pltpu.SemaphoreType.DMA((3,)),
            pltpu.VMEM((D_MODEL, 2 * D_RNN), BF16),
            pltpu.VMEM((N_GATE_GROUPS, GATE_GROUP_W, 2 * GATE_GROUP_W), BF16),
            pltpu.VMEM((D_RNN, D_MODEL), BF16),
            pltpu.VMEM((LRU_TS, D_MODEL), BF16),
            pltpu.VMEM((LRU_TS + (CONV_W - 1) * SUBLANES, D_RNN), F32),
            pltpu.VMEM((LRU_TS, D_RNN), F32),
            pltpu.VMEM((LRU_TS, D_RNN), F32),
            pltpu.VMEM((LRU_TS, D_RNN), F32),
            pltpu.VMEM((LRU_TS, D_RNN), BF16),
            pltpu.VMEM((LRU_TS, 2 * D_RNN), F32),
            pltpu.VMEM((LRU_TS, D_RNN), BF16),
            pltpu.VMEM(((CONV_W - 1) * SUBLANES, D_RNN), F32),
            pltpu.VMEM((SUBLANES, D_RNN), F32),
            pltpu.VMEM((SUBLANES, D_RNN), F32),
            pltpu.VMEM((SUBLANES, D_RNN), F32),
            pltpu.VMEM((2, SUBLANES, D_RNN), F32),
        ],
        compiler_params=pltpu.CompilerParams(
            dimension_semantics=("arbitrary", "arbitrary"), vmem_limit_bytes=VMEM_LIMIT),
        name="lru_layer",
    )(x, _rows8(gpre), _rows8(gpost), w_in, jnp.repeat(0.5 * conv_w, SUBLANES, axis=0), _rows8(0.5 * conv_b),
      wg, _rows8(0.5 * b_a), _rows8(0.5 * b_x), _rows8(lam), w_out)


def _ret_layer(wj, x, gpre, gpost, w_in, w_out, cos, sin, dmask, qdec, kdec, cdec):
    bsz, s, _ = x.shape
    ts = RET_TS
    x_spec = pl.BlockSpec((1, ts, D_MODEL), lambda b, t: (b, t, 0))
    rope_spec = pl.BlockSpec((ts, QK_HEAD // 2), lambda b, t: (t, 0))
    return pl.pallas_call(
        functools.partial(_ret_kernel, wj),
        grid=(bsz, s // ts),
        in_specs=[pl.BlockSpec(memory_space=pl.ANY), _const_spec((1, D_MODEL)), _const_spec((1, D_MODEL)),
                  pl.BlockSpec(memory_space=pl.ANY), pl.BlockSpec(memory_space=pl.ANY),
                  rope_spec, rope_spec,
                  _const_spec((RET_HEADS, CHUNK, CHUNK)), _const_spec((RET_HEADS, CHUNK, QK_HEAD // 2)),
                  _const_spec((RET_HEADS, CHUNK, QK_HEAD // 2)),
                  pl.BlockSpec(memory_space=pltpu.SMEM)],
        out_specs=x_spec,
        out_shape=jax.ShapeDtypeStruct(x.shape, x.dtype),
        scratch_shapes=[
            pltpu.VMEM((3, ts, D_MODEL), F32),
            pltpu.SemaphoreType.DMA((3,)),
            pltpu.SemaphoreType.DMA((2,)),
            pltpu.VMEM((RET_HEADS, D_MODEL, HEAD_COLS), BF16),
            pltpu.VMEM((V_DIM, D_MODEL), BF16),
            pltpu.VMEM((ts, D_MODEL), BF16),
            pltpu.VMEM((2, ts, HEAD_COLS), F32),
            pltpu.VMEM((ts, V_DIM), BF16),
            pltpu.VMEM((RET_HEADS, QK_HEAD, V_HEAD), F32),
        ],
        compiler_params=pltpu.CompilerParams(
            dimension_semantics=("arbitrary", "arbitrary"), vmem_limit_bytes=VMEM_LIMIT),
        name="ret_layer",
    )(x, gpre, gpost, w_in, w_out, cos, sin, dmask, qdec, kdec, cdec)


def _gate_group_weights(w_a, w_x):
    def bd(w):
        n = w.shape[0]
        w = w.reshape(n, N_GATE_GROUPS, GATE_GROUP_BLOCKS, BLOCK_W, BLOCK_W)
        eye = jnp.eye(GATE_GROUP_BLOCKS, dtype=w.dtype)
        full = w[:, :, :, :, None, :] * eye[None, None, :, None, :, None]
        return full.reshape(n, N_GATE_GROUPS, GATE_GROUP_W, GATE_GROUP_W)
    return jnp.concatenate([bd(w_a), bd(w_x)], axis=-1).astype(BF16)


def _retention_tables(s):
    pos = jnp.arange(s, dtype=F32)
    inv_freq = ROPE_BASE ** (-jnp.arange(0, QK_HEAD, 2, dtype=F32) / QK_HEAD)
    ang = pos[:, None] * inv_freq[None, :]
    cos, sin = jnp.cos(ang), jnp.sin(ang)
    log_g = jnp.log(1.0 - jnp.exp2(-5.0 - jnp.arange(RET_HEADS, dtype=F32)))
    idx = jnp.arange(CHUNK, dtype=F32)
    diff = idx[:, None] - idx[None, :]
    dmask = jnp.where(diff[None] >= 0, jnp.exp(jnp.maximum(diff, 0.0)[None] * log_g[:, None, None]), 0.0)
    q_decay = jnp.exp((idx[None, :] + 1.0) * log_g[:, None])[..., None]
    k_decay = jnp.exp((CHUNK - 1.0 - idx[None, :]) * log_g[:, None])[..., None]
    qdec = jnp.broadcast_to(q_decay, (RET_HEADS, CHUNK, QK_HEAD // 2))
    kdec = jnp.broadcast_to(k_decay, (RET_HEADS, CHUNK, QK_HEAD // 2))
    cdec = jnp.exp(CHUNK * log_g)
    return cos, sin, dmask, qdec, kdec, cdec


def kernel(x, norm_pre, norm_post, lru_w_in, lru_conv_w, lru_conv_b, lru_w_a, lru_b_a, lru_w_x, lru_b_x,
           lru_lambda, lru_w_out, ret_w_in, ret_w_out):
    s = x.shape[1]
    tables = _retention_tables(s)
    lru_w = (lru_w_in.astype(BF16), _gate_group_weights(lru_w_a, lru_w_x), lru_w_out.astype(BF16))
    ret_w = (ret_w_in.astype(BF16), ret_w_out.astype(BF16))
    for layer in range(DEPTH):
        j = layer // 2
        if layer % 2 == 0:
            x = _lru_layer(j, x, norm_pre[layer], norm_post[layer], lru_w[0], lru_conv_w[j], lru_conv_b[j],
                           lru_w[1], lru_b_a[j], lru_b_x[j], lru_lambda[j], lru_w[2])
        else:
            x = _ret_layer(j, x, norm_pre[layer][None, :], norm_post[layer][None, :], ret_w[0], ret_w[1], *tables)
    return x
```
